```python
import math
import jax, jax.numpy as jnp
from jax import lax
import numpy as np

D_MODEL = 4096
BATCH = 4
SEQ = 2048
DEPTH = 4
DEC_BATCH = 8
DEC_SEQ = 1
PAST_LEN = 8192
PAGE_SIZE = 128

N_A = DEPTH // 2
N_B = DEPTH - N_A
A_HEADS = 32
A_DK = D_MODEL // A_HEADS
A_DV = D_MODEL // A_HEADS
A_DQK = A_HEADS * A_DK
A_DVW = A_HEADS * A_DV
QKV_W = 2 * A_DQK + A_DVW
A_PROJ_W = QKV_W + A_DVW + 2 * A_HEADS
CONV_W = 4
CHUNK = 64
B_HEADS = 16
B_DH = D_MODEL // (2 * B_HEADS)
B_DV = 2 * B_DH
Q_BLOCK = 128
ROPE_THETA = 10000.0
D_FF = ((8 * D_MODEL // 3 + 255) // 256) * 256
FFN_CONV_W = 3
EPS = 1e-6

kernel_name = 'yoco_gdn_diffattn_convffn_step'


def rmsnorm(x, g):
    xf = x.astype(jnp.float32)
    y = xf * lax.rsqrt(jnp.mean(xf * xf, axis=-1, keepdims=True) + EPS)
    return (y * g.astype(jnp.float32)).astype(x.dtype)


def l2norm(x):
    xf = x.astype(jnp.float32)
    return (xf * lax.rsqrt(jnp.sum(xf * xf, axis=-1, keepdims=True) + EPS)).astype(x.dtype)


def modulate(h, shift, scale):
    return h * (1.0 + scale[:, None]) + shift[:, None]


def rope(x, pos):
    dh = x.shape[-1]
    half = dh // 2
    inv = ROPE_THETA ** (-jnp.arange(half, dtype=jnp.float32) / half)
    ang = pos.astype(jnp.float32)[:, None] * inv
    ang = ang.reshape((1, pos.shape[0]) + (1,) * (x.ndim - 3) + (half,))
    cos, sin = jnp.cos(ang), jnp.sin(ang)
    xf = x.astype(jnp.float32)
    x1, x2 = xf[..., :half], xf[..., half:]
    return jnp.concatenate([x1 * cos - x2 * sin, x2 * cos + x1 * sin], axis=-1).astype(x.dtype)


def causal_dwconv(x, buf, w):
    W = w.shape[0]
    T = x.shape[1]
    xp = jnp.concatenate([buf.astype(x.dtype), x], axis=1)
    y = xp[:, 0:T] * w[0]
    for i in range(1, W):
        y = y + xp[:, i:i + T] * w[i]
    return y, xp[:, T:]


def gated_delta_rule(q, k, v, g, beta, S0):
    Bn, T, H, dk = q.shape
    dv = v.shape[-1]
    C = math.gcd(T, CHUNK)
    n = T // C
    f32 = jnp.float32

    def blk(a):
        a = a.astype(f32).reshape((Bn, n, C, H) + a.shape[3:])
        return jnp.moveaxis(a, 3, 1)

    qc, kc, vc = blk(q), blk(k), blk(v)
    gc = jnp.cumsum(blk(g), axis=-1)
    bc = blk(beta)
    idx = jnp.arange(C)
    incl = idx[:, None] >= idx[None, :]
    strict = idx[:, None] > idx[None, :]
    dec = jnp.exp(jnp.where(incl, gc[..., :, None] - gc[..., None, :], -jnp.inf))
    a_kk = jnp.where(strict, jnp.einsum('bhnid,bhnjd->bhnij', kc, kc) * dec, 0.0) * bc[..., :, None]
    rhs = jnp.concatenate([vc * bc[..., None], kc * (bc * jnp.exp(gc))[..., None]], axis=-1)
    sol = lax.linalg.triangular_solve(a_kk, rhs, left_side=True, lower=True, unit_diagonal=True)
    u_v, w_k = sol[..., :dv], sol[..., dv:]
    a_qk = jnp.einsum('bhnid,bhnjd->bhnij', qc, kc) * dec
    q_dec = qc * jnp.exp(gc)[..., None]
    g_last = gc[..., -1]
    k_dec = kc * jnp.exp(g_last[..., None] - gc)[..., None]
    xs = tuple(jnp.moveaxis(t, 2, 0) for t in (q_dec, k_dec, w_k, u_v, a_qk, jnp.exp(g_last)))

    def step(S, xc):
        qd, kd, wk, uv, aqk, gt = xc
        v_new = uv - jnp.einsum('bhcd,bhde->bhce', wk, S)
        o = jnp.einsum('bhcd,bhde->bhce', qd, S) + jnp.einsum('bhij,bhje->bhie', aqk, v_new)
        S = gt[..., None, None] * S + jnp.einsum('bhcd,bhce->bhde', kd, v_new)
        return S, o

    S_fin, o = lax.scan(step, S0.astype(f32), xs)
    o = jnp.transpose(o, (1, 0, 3, 2, 4)).reshape(Bn, T, H, dv)
    return o.astype(v.dtype), S_fin.astype(S0.dtype)


def delta_mixer(h, conv_buf, S0, w_in, conv_w, a_log, dt_bias, g_out, w_out):
    Bn, T, _ = h.shape
    proj = h @ w_in
    qkv = proj[..., :QKV_W]
    gate = proj[..., QKV_W:QKV_W + A_DVW]
    a = proj[..., QKV_W + A_DVW:QKV_W + A_DVW + A_HEADS]
    b = proj[..., QKV_W + A_DVW + A_HEADS:]
    qkv, new_buf = causal_dwconv(qkv, conv_buf, conv_w)
    qkv = jax.nn.silu(qkv)
    q = l2norm(qkv[..., :A_DQK].reshape(Bn, T, A_HEADS, A_DK)) * (A_DK ** -0.5)
    k = l2norm(qkv[..., A_DQK:2 * A_DQK].reshape(Bn, T, A_HEADS, A_DK))
    v = qkv[..., 2 * A_DQK:].reshape(Bn, T, A_HEADS, A_DV)
    beta = jax.nn.sigmoid(b.astype(jnp.float32))
    g = -jnp.exp(a_log.astype(jnp.float32)) * jax.nn.softplus(a.astype(jnp.float32) + dt_bias.astype(jnp.float32))
    o, S = gated_delta_rule(q, k, v, g, beta, S0)
    o = rmsnorm(o, g_out) * jax.nn.silu(gate.reshape(Bn, T, A_HEADS, A_DV))
    return o.reshape(Bn, T, A_DVW) @ w_out, new_buf, S


def diff_attend(q, k, v, lam, q_pos, k_pos):
    Bn, Tq, H, _, dh = q.shape
    dv = v.shape[-1]
    QB = math.gcd(Tq, Q_BLOCK)
    nb = Tq // QB
    qb = jnp.moveaxis(q.reshape(Bn, nb, QB, H, 2, dh), 1, 0)
    pb = q_pos.reshape(nb, QB)
    scale = dh ** -0.5

    def one(args):
        qi, pi = args
        s = jnp.einsum('bqhmd,bkhmd->bhmqk', qi, k).astype(jnp.float32) * scale
        mask = k_pos[None, :] <= pi[:, None]
        p = jax.nn.softmax(jnp.where(mask, s, -jnp.inf), axis=-1)
        w = p[:, :, 0] - lam * p[:, :, 1]
        return jnp.einsum('bhqk,bkhe->bqhe', w.astype(v.dtype), v)

    o = lax.map(one, (qb, pb))
    return jnp.moveaxis(o, 0, 1).reshape(Bn, Tq, H, dv)


def diff_mixer(h, k_all, v_all, q_pos, k_pos, w_q, lam_vec, g_sub, w_o, lam_init):
    Bn, T, _ = h.shape
    Tk = k_all.shape[1]
    q = rope((h @ w_q).reshape(Bn, T, B_HEADS, 2, B_DH), q_pos)
    lv = lam_vec.astype(jnp.float32)
    lam = jnp.exp(jnp.sum(lv[0] * lv[1])) - jnp.exp(jnp.sum(lv[2] * lv[3])) + lam_init
    o = diff_attend(q, k_all.reshape(Bn, Tk, B_HEADS, 2, B_DH), v_all, lam, q_pos, k_pos)
    o = rmsnorm(o, g_sub) * (1.0 - lam_init)
    return o.reshape(Bn, T, B_HEADS * B_DV) @ w_o


def conv_ffn(h, buf, w_gate, w_up, conv_w, conv_b, w_down):
    hg, new_buf = causal_dwconv(h @ w_gate, buf, conv_w)
    a = jax.nn.silu(hg + conv_b) * (h @ w_up)
    return a @ w_down, new_buf


def lambda_init(layer):
    return 0.8 - 0.6 * math.exp(-0.3 * layer)


def setup_inputs(seed: int = 0) -> dict:
    key = jax.random.key(seed)
    ks = jax.random.split(key, 64)
    counter = iter(range(64))
    f32 = jnp.float32

    def nk():
        return ks[next(counter)]

    def nrm(shape, std):
        return std * jax.random.normal(nk(), shape, f32)

    def gain(shape):
        return 1.0 + nrm(shape, 0.05)

    n_pages = PAST_LEN // PAGE_SIZE
    n_used = DEC_BATCH * n_pages
    n_pool = n_used + n_used // 4
    page_table = jax.random.permutation(nk(), n_pool)[:n_used].reshape(DEC_BATCH, n_pages).astype(jnp.int32)
    sd = D_MODEL ** -0.5
    dt = jnp.exp(math.log(1e-3) + jax.random.uniform(nk(), (N_A, A_HEADS), f32) * (math.log(1e-1) - math.log(1e-3)))
    return {
        'x_prompt': nrm((BATCH, SEQ, D_MODEL), 1.0),
        'x_sample': nrm((DEC_BATCH, DEC_SEQ, D_MODEL), 1.0),
        'cache_k': nrm((n_pool, PAGE_SIZE, B_HEADS, 2 * B_DH), 1.0),
        'cache_v': nrm((n_pool, PAGE_SIZE, B_HEADS, B_DV), 1.0),
        'state_delta': nrm((N_A, DEC_BATCH, A_HEADS, A_DK, A_DV), 0.1),
        'state_qkv_conv': nrm((N_A, DEC_BATCH, CONV_W - 1, QKV_W), 1.0),
        'state_ffn_conv': nrm((DEPTH, DEC_BATCH, FFN_CONV_W - 1, D_FF), 1.0),
        'page_table': page_table,
        'c_prompt': nrm((BATCH, D_MODEL), 1.0),
        'c_sample': nrm((DEC_BATCH, D_MODEL), 1.0),
        'w_ada': nrm((DEPTH, D_MODEL, 6 * D_MODEL), 0.5 * sd),
        'b_ada': nrm((DEPTH, 6 * D_MODEL), 0.02),
        'g_mix': gain((DEPTH, D_MODEL)),
        'g_ffn': gain((DEPTH, D_MODEL)),
        'w_in_a': nrm((N_A, D_MODEL, A_PROJ_W), sd),
        'conv_a': nrm((N_A, CONV_W, QKV_W), CONV_W ** -0.5),
        'a_log': jnp.log(jax.random.uniform(nk(), (N_A, A_HEADS), f32, 1.0, 16.0)),
        'dt_bias': dt + jnp.log(-jnp.expm1(-dt)),
        'g_out_a': gain((N_A, A_DV)),
        'w_out_a': nrm((N_A, A_DVW, D_MODEL), A_DVW ** -0.5),
        'w_ada_kv': nrm((D_MODEL, 2 * D_MODEL), 0.5 * sd),
        'b_ada_kv': nrm((2 * D_MODEL,), 0.02),
        'g_kv': gain((D_MODEL,)),
        'w_k': nrm((D_MODEL, B_HEADS * 2 * B_DH), sd),
        'w_v': nrm((D_MODEL, B_HEADS * B_DV), sd),
        'w_q_b': nrm((N_B, D_MODEL, B_HEADS * 2 * B_DH), sd),
        'lam_b': nrm((N_B, 4, B_DH), 0.1),
        'g_sub_b': gain((N_B, B_DV)),
        'w_o_b': nrm((N_B, B_HEADS * B_DV, D_MODEL), (B_HEADS * B_DV) ** -0.5),
        'w_gate': nrm((DEPTH, D_MODEL, D_FF), sd),
        'w_up': nrm((DEPTH, D_MODEL, D_FF), sd),
        'conv_f': nrm((DEPTH, FFN_CONV_W, D_FF), FFN_CONV_W ** -0.5),
        'b_conv_f': nrm((DEPTH, D_FF), 0.02),
        'w_down': nrm((DEPTH, D_FF, D_MODEL), D_FF ** -0.5),
        'g_final': gain((D_MODEL,)),
    }


def reference(x_prompt, x_sample, cache_k, cache_v, state_delta, state_qkv_conv, state_ffn_conv, page_table,
              c_prompt, c_sample, w_ada, b_ada, g_mix, g_ffn, w_in_a, conv_a, a_log, dt_bias, g_out_a, w_out_a,
              w_ada_kv, b_ada_kv, g_kv, w_k, w_v, w_q_b, lam_b, g_sub_b, w_o_b,
              w_gate, w_up, conv_f, b_conv_f, w_down, g_final):

    def trunk(x, c, k_past, v_past, S_a, qkv_bufs, ffn_bufs):
        Bn, T, _ = x.shape
        P = k_past.shape[1]
        pos = P + jnp.arange(T, dtype=jnp.int32)
        k_pos = jnp.arange(P + T, dtype=jnp.int32)
        sc = jax.nn.silu(c)
        new_S, new_qkv, new_ffn = [], [], []
        k_new = v_new = k_all = v_all = None
        for l in range(DEPTH):
            mod = sc @ w_ada[l] + b_ada[l]
            sh_m, sc_m, gt_m, sh_f, sc_f, gt_f = jnp.split(mod, 6, axis=-1)
            h = modulate(rmsnorm(x, g_mix[l]), sh_m, sc_m)
            if l < N_A:
                out, buf, S = delta_mixer(h, qkv_bufs[l], S_a[l], w_in_a[l], conv_a[l], a_log[l], dt_bias[l],
                                          g_out_a[l], w_out_a[l])
                new_qkv.append(buf)
                new_S.append(S)
            else:
                j = l - N_A
                out = diff_mixer(h, k_all, v_all, pos, k_pos, w_q_b[j], lam_b[j], g_sub_b[j], w_o_b[j],
                                 lambda_init(l))
            x = x + gt_m[:, None] * out
            h = modulate(rmsnorm(x, g_ffn[l]), sh_f, sc_f)
            out, fbuf = conv_ffn(h, ffn_bufs[l], w_gate[l], w_up[l], conv_f[l], b_conv_f[l], w_down[l])
            new_ffn.append(fbuf)
            x = x + gt_f[:, None] * out
            if l == N_A - 1:
                sh_kv, sc_kv = jnp.split(sc @ w_ada_kv + b_ada_kv, 2, axis=-1)
                hkv = modulate(rmsnorm(x, g_kv), sh_kv, sc_kv)
                k_new = rope((hkv @ w_k).reshape(Bn, T, B_HEADS, 2, B_DH), pos).reshape(Bn, T, B_HEADS, 2 * B_DH)
                v_new = (hkv @ w_v).reshape(Bn, T, B_HEADS, B_DV)
                k_all = jnp.concatenate([k_past.astype(k_new.dtype), k_new], axis=1)
                v_all = jnp.concatenate([v_past.astype(v_new.dtype), v_new], axis=1)
        y = rmsnorm(x, g_final)
        return y, k_new, v_new, jnp.stack(new_S), jnp.stack(new_qkv), jnp.stack(new_ffn)

    dt_ = x_prompt.dtype
    bp = x_prompt.shape[0]
    yp, kp, vp, Sp, qcp, fcp = trunk(
        x_prompt, c_prompt,
        jnp.zeros((bp, 0, B_HEADS, 2 * B_DH), dt_), jnp.zeros((bp, 0, B_HEADS, B_DV), dt_),
        jnp.zeros((N_A, bp, A_HEADS, A_DK, A_DV), state_delta.dtype),
        jnp.zeros((N_A, bp, CONV_W - 1, QKV_W), dt_),
        jnp.zeros((DEPTH, bp, FFN_CONV_W - 1, D_FF), dt_))
    bs = x_sample.shape[0]
    n_pages = PAST_LEN // PAGE_SIZE
    k_past = cache_k[page_table].reshape(bs, n_pages * PAGE_SIZE, B_HEADS, 2 * B_DH)
    v_past = cache_v[page_table].reshape(bs, n_pages * PAGE_SIZE, B_HEADS, B_DV)
    ys, ks_, vs_, Ss, qcs, fcs = trunk(x_sample, c_sample, k_past, v_past, state_delta, state_qkv_conv, state_ffn_conv)
    return (yp, ys, kp, vp, ks_, vs_, Sp, Ss, qcp, qcs, fcp, fcs)
```

```python
import functools
import math

import jax
import jax.numpy as jnp
from jax import lax
from jax.experimental import pallas as pl
from jax.experimental.pallas import tpu as pltpu

F32 = jnp.float32
BF16 = jnp.bfloat16
DT_KK = BF16
DT_STATE = BF16

D_MODEL = 4096
DEPTH = 4
N_A = DEPTH // 2
A_HEADS = 32
A_DK = 128
A_DV = 128
A_DQK = A_HEADS * A_DK
A_DVW = A_HEADS * A_DV
QKV_W = 2 * A_DQK + A_DVW
CONV_W = 4
CHUNK = 64
B_HEADS = 16
B_DH = 128
B_DV = 256
ROPE_THETA = 10000.0
D_FF = 11008
FFN_CONV_W = 3
EPS = 1e-6
PAGE_SIZE = 128

VMEM_LIMIT_BYTES = 60000 * 1024
LANES = 128


def _params(sem):
    return pltpu.CompilerParams(dimension_semantics=sem, vmem_limit_bytes=VMEM_LIMIT_BYTES)


def _dot(a, b):
    return jnp.dot(a, b, preferred_element_type=F32)


def _dot_f32(a, b):
    return jnp.dot(a, b, preferred_element_type=F32, precision=lax.Precision.HIGHEST)


def _dot_nt(a, b):
    return lax.dot_general(a, b, (((1,), (1,)), ((), ())), preferred_element_type=F32)


def _dot_tn(a, b):
    return lax.dot_general(a, b, (((0,), (0,)), ((), ())), preferred_element_type=F32)


def _silu(x):
    return x * jax.nn.sigmoid(x)


def _wspec(w, layer, k, tn, off_blocks):
    if w.ndim == 3:
        return pl.BlockSpec((None, k, tn), lambda i, j: (layer, 0, j + off_blocks))
    return pl.BlockSpec((k, tn), lambda i, j: (0, j + off_blocks))


def _mod_body(c_ref, w_ref, b_ref, o_ref):
    c = c_ref[...]
    o_ref[...] = _dot(_silu(c).astype(BF16), w_ref[...].astype(BF16)) + b_ref[...]


def _mod_all(c_all, w, b, tn=512):
    nl, k, n = w.shape
    mp = c_all.shape[0]
    return pl.pallas_call(
        _mod_body,
        grid=(nl, n // tn),
        in_specs=[
            pl.BlockSpec((mp, k), lambda l, j: (0, 0)),
            pl.BlockSpec((None, k, tn), lambda l, j: (l, 0, j)),
            pl.BlockSpec((None, 1, tn), lambda l, j: (l, 0, j)),
        ],
        out_specs=pl.BlockSpec((None, mp, tn), lambda l, j: (l, 0, j)),
        out_shape=jax.ShapeDtypeStruct((nl, mp, n), F32),
        compiler_params=_params(("parallel", "parallel")),
        name="adaln_mod",
    )(c_all, w, b.reshape(nl, 1, n))


def _norm_mod_body(x_ref, g_ref, sh_ref, sc_ref, o_ref):
    x = x_ref[...]
    y = x * lax.rsqrt(jnp.mean(x * x, axis=-1, keepdims=True) + EPS) * g_ref[...]
    o_ref[...] = (y * (1.0 + sc_ref[...]) + sh_ref[...]).astype(o_ref.dtype)


def _norm_body(x_ref, g_ref, o_ref):
    x = x_ref[...]
    o_ref[...] = (x * lax.rsqrt(jnp.mean(x * x, axis=-1, keepdims=True) + EPS) * g_ref[...]).astype(o_ref.dtype)


def _norm_mod(x3, g, layer, mod3, k_shift, k_scale, tm, out_dtype):
    b, t, d = x3.shape
    g3 = g.reshape(-1, 1, d)
    r = mod3.shape[1]
    rb = 1 if r == 1 else tm
    mspec = lambda k: pl.BlockSpec((None, rb, d), lambda bi, i: (bi, 0 if r == 1 else i, k))
    return pl.pallas_call(
        _norm_mod_body,
        grid=(b, t // tm),
        in_specs=[
            pl.BlockSpec((None, tm, d), lambda bi, i: (bi, i, 0)),
            pl.BlockSpec((None, 1, d), lambda bi, i: (layer, 0, 0)),
            mspec(k_shift),
            mspec(k_scale),
        ],
        out_specs=pl.BlockSpec((None, tm, d), lambda bi, i: (bi, i, 0)),
        out_shape=jax.ShapeDtypeStruct((b, t, d), out_dtype),
        compiler_params=_params(("parallel", "parallel")),
        name="norm_mod",
    )(x3, g3, mod3, mod3)


def _final_norm(x3, g, tm):
    b, t, d = x3.shape
    return pl.pallas_call(
        _norm_body,
        grid=(b, t // tm),
        in_specs=[
            pl.BlockSpec((None, tm, d), lambda bi, i: (bi, i, 0)),
            pl.BlockSpec((1, d), lambda bi, i: (0, 0)),
        ],
        out_specs=pl.BlockSpec((None, tm, d), lambda bi, i: (bi, i, 0)),
        out_shape=jax.ShapeDtypeStruct((b, t, d), F32),
        compiler_params=_params(("parallel", "parallel")),
        name="final_norm",
    )(x3, g.reshape(1, d))


def _mm_body(x_ref, w_ref, *rest, epi, scale):
    acc = _dot(x_ref[...].astype(BF16), w_ref[...].astype(BF16))
    if epi == "plain":
        (o_ref,) = rest
        out = acc
    elif epi == "resid":
        res_ref, gt_ref, o_ref = rest
        out = res_ref[...] + gt_ref[...] * acc
    else:
        cs_ref, sn_ref, o_ref = rest
        cs = cs_ref[...]
        sn = sn_ref[...]
        parts = []
        for gi in range(acc.shape[1] // LANES):
            a = acc[:, gi * LANES:(gi + 1) * LANES]
            parts.append(a * cs + pltpu.roll(a, LANES // 2, axis=1) * sn)
        out = jnp.concatenate(parts, axis=1)
        if scale != 1.0:
            out = out * scale
    o_ref[...] = out.astype(o_ref.dtype)


def _matmul(x, w, *, layer=0, col_off=0, n_out, tm, tn, rows_per_seq, epi="plain", res=None, mod3=None,
            k_gate=0, rope=None, scale=1.0, out_dtype=F32, name="matmul"):
    m, k = x.shape
    tps = rows_per_seq // tm
    in_specs = [pl.BlockSpec((tm, k), lambda i, j: (i, 0)), _wspec(w, layer, k, tn, col_off // tn)]
    args = [x, w]
    if epi == "resid":
        r = mod3.shape[1]
        rb = 1 if r == 1 else tm
        goff = k_gate * n_out // tn
        in_specs += [
            pl.BlockSpec((tm, tn), lambda i, j: (i, j)),
            pl.BlockSpec((None, rb, tn), lambda i, j: (i // tps, 0 if r == 1 else i % tps, goff + j)),
        ]
        args += [res, mod3]
    elif epi == "rope":
        in_specs += [pl.BlockSpec((tm, LANES), lambda i, j: (i % tps, 0))] * 2
        args += list(rope)
    return pl.pallas_call(
        functools.partial(_mm_body, epi=epi, scale=scale),
        grid=(m // tm, n_out // tn),
        in_specs=in_specs,
        out_specs=pl.BlockSpec((tm, tn), lambda i, j: (i, j)),
        out_shape=jax.ShapeDtypeStruct((m, n_out), out_dtype),
        compiler_params=_params(("parallel", "parallel")),
        name=name,
    )(*args)


def _ffn1_body(x_ref, wg_ref, wu_ref, cw_ref, cb_ref, p_ref, a_ref, nb_ref, *scratch, tps, seq_rows):
    i = pl.program_id(0)
    j = pl.program_id(1)
    x = x_ref[...].astype(BF16)
    g = _dot(x, wg_ref[...].astype(BF16))
    u = _dot(x, wu_ref[...].astype(BF16))
    cw = cw_ref[...]
    tm, tn = g.shape
    if seq_rows:
        (carry_ref,) = scratch

        @pl.when(i == 0)
        def _():
            carry_ref[j] = jnp.zeros((16, tn), F32)

        prev = jnp.where((i % tps) == 0, p_ref[...], carry_ref[j])
        r16 = lax.broadcasted_iota(jnp.int32, (16, tn), 0)
        head = g[0:16]

        def shifted(s):
            h16 = jnp.where(r16 < s, pltpu.roll(prev, s, axis=0), pltpu.roll(head, s, axis=0))
            return jnp.concatenate([h16, pltpu.roll(g, s, axis=0)[16:]], axis=0)

        g1 = shifted(1)
        g2 = shifted(2)
        carry_ref[j] = g[tm - 16:tm]
        nb_ref[...] = g[tm - 8:tm]
    else:
        g1 = p_ref[1]
        g2 = p_ref[0]
        nb_ref[...] = g
    y = cw[2:3] * g + cw[1:2] * g1 + cw[0:1] * g2 + cb_ref[...]
    a_ref[...] = (_silu(y) * u).astype(a_ref.dtype)


def _ffn1(h, w_gate, w_up, conv_f, b_conv_f, layer, prev, *, tm, tn, rows_per_seq, seq_rows):
    m, k = h.shape
    n = D_FF
    tps = rows_per_seq // tm
    nb_rows = 8 if seq_rows else m
    if seq_rows:
        pspec = pl.BlockSpec((None, 16, tn), lambda i, j: (i // tps, 0, j))
        scratch = [pltpu.VMEM((n // tn, 16, tn), F32)]
    else:
        pspec = pl.BlockSpec((2, m, tn), lambda i, j: (0, 0, j))
        scratch = []
    return pl.pallas_call(
        functools.partial(_ffn1_body, tps=tps, seq_rows=seq_rows),
        grid=(m // tm, n // tn),
        in_specs=[
            pl.BlockSpec((tm, k), lambda i, j: (i, 0)),
            pl.BlockSpec((None, k, tn), lambda i, j: (layer, 0, j)),
            pl.BlockSpec((None, k, tn), lambda i, j: (layer, 0, j)),
            pl.BlockSpec((None, FFN_CONV_W, tn), lambda i, j: (layer, 0, j)),
            pl.BlockSpec((None, 1, tn), lambda i, j: (layer, 0, j)),
            pspec,
        ],
        out_specs=[
            pl.BlockSpec((tm, tn), lambda i, j: (i, j)),
            pl.BlockSpec((None, nb_rows, tn), lambda i, j: (i, 0, j)),
        ],
        out_shape=[
            jax.ShapeDtypeStruct((m, n), BF16 if seq_rows else F32),
            jax.ShapeDtypeStruct((m // tm, nb_rows, n), F32),
        ],
        scratch_shapes=scratch,
        compiler_params=_params(("arbitrary", "arbitrary")),
        name="ffn_gate_up",
    )(h, w_gate, w_up, conv_f, b_conv_f.reshape(DEPTH, 1, n), prev)


def _qkvconv_body(x_ref, cw_ref, p_ref, o_ref, *scratch, tps, seq_rows, q_blocks, qk_blocks):
    jn = pl.program_id(0)
    i = pl.program_id(1)
    x = x_ref[...]
    cw = cw_ref[...]
    tm, tn = x.shape
    if seq_rows:
        (carry_ref,) = scratch

        @pl.when(i == 0)
        def _():
            carry_ref[...] = jnp.zeros((8, tn), F32)

        prev = jnp.where((i % tps) == 0, p_ref[...], carry_ref[...])
        r8 = lax.broadcasted_iota(jnp.int32, (8, tn), 0)
        head = x[0:8]

        def shifted(s):
            h8 = jnp.where(r8 < s, pltpu.roll(prev, s, axis=0), pltpu.roll(head, s, axis=0))
            return jnp.concatenate([h8, pltpu.roll(x, s, axis=0)[8:]], axis=0)

        x1, x2, x3 = shifted(1), shifted(2), shifted(3)
        carry_ref[...] = x[tm - 8:tm]
    else:
        x1, x2, x3 = p_ref[2], p_ref[1], p_ref[0]
    y = _silu(cw[3:4] * x + cw[2:3] * x1 + cw[1:2] * x2 + cw[0:1] * x3)
    mult = jnp.where(jn < q_blocks, A_DK ** -0.5, 1.0)
    is_qk = jn < qk_blocks
    parts = []
    for gi in range(tn // LANES):
        z = y[:, gi * LANES:(gi + 1) * LANES]
        nrm = lax.rsqrt(jnp.sum(z * z, axis=-1, keepdims=True) + EPS) * mult
        parts.append(z * jnp.where(is_qk, nrm, 1.0))
    o_ref[...] = jnp.concatenate(parts, axis=1)


def _qkvconv(proj, conv_a, layer, prev, *, tm, tn, rows_per_seq, seq_rows):
    m = proj.shape[0]
    tps = rows_per_seq // tm
    if seq_rows:
        pspec = pl.BlockSpec((None, 8, tn), lambda jn, i: (i // tps, 0, jn))
        scratch = [pltpu.VMEM((8, tn), F32)]
    else:
        pspec = pl.BlockSpec((CONV_W - 1, m, tn), lambda jn, i: (0, 0, jn))
        scratch = []
    return pl.pallas_call(
        functools.partial(_qkvconv_body, tps=tps, seq_rows=seq_rows, q_blocks=A_DQK // tn,
                          qk_blocks=2 * A_DQK // tn),
        grid=(QKV_W // tn, m // tm),
        in_specs=[
            pl.BlockSpec((tm, tn), lambda jn, i: (i, jn)),
            pl.BlockSpec((None, CONV_W, tn), lambda jn, i: (layer, 0, jn)),
            pspec,
        ],
        out_specs=pl.BlockSpec((tm, tn), lambda jn, i: (i, jn)),
        out_shape=jax.ShapeDtypeStruct((m, QKV_W), F32),
        scratch_shapes=scratch,
        compiler_params=_params(("arbitrary", "arbitrary")),
        name="qkv_conv",
    )(proj, conv_a, prev)


def _gates_body(ab_ref, alog_ref, dtb_ref, g_ref, beta_ref, *, cumsum):
    ab = ab_ref[...]
    x = ab + dtb_ref[...]
    softplus = jnp.maximum(x, 0.0) + jnp.log1p(jnp.exp(-jnp.abs(x)))
    g = -jnp.exp(alog_ref[...]) * softplus
    if cumsum:
        r = lax.broadcasted_iota(jnp.int32, g.shape, 0) & (CHUNK - 1)
        s = 1
        while s < CHUNK:
            g = g + jnp.where(r >= s, pltpu.roll(g, s, axis=0), 0.0)
            s *= 2
    g_ref[...] = g
    beta_ref[...] = jax.nn.sigmoid(ab)


def _gates(ab, a_log, dt_bias, *, tm, cumsum):
    m = ab.shape[0]
    pad = lambda v: jnp.pad(v.astype(F32), (0, LANES - A_HEADS)).reshape(1, LANES)
    return pl.pallas_call(
        functools.partial(_gates_body, cumsum=cumsum),
        grid=(m // tm,),
        in_specs=[
            pl.BlockSpec((tm, LANES), lambda i: (i, 0)),
            pl.BlockSpec((1, LANES), lambda i: (0, 0)),
            pl.BlockSpec((1, LANES), lambda i: (0, 0)),
        ],
        out_specs=[pl.BlockSpec((tm, LANES), lambda i: (i, 0))] * 2,
        out_shape=[jax.ShapeDtypeStruct((m, LANES), F32)] * 2,
        compiler_params=_params(("parallel",)),
        name="delta_gates",
    )(ab, pad(a_log), pad(dt_bias))


def _delta_body(q_ref, k_ref, v_ref, gate_ref, gc_ref, gct_ref, beta_ref, s0_ref, gout_ref, og_ref, s_ref, *,
                heads, chunks):
    @pl.when(pl.program_id(2) == 0)
    def _():
        s_ref[...] = s0_ref[...]

    c = CHUNK
    ii = lax.broadcasted_iota(jnp.int32, (c, c), 0)
    jj = lax.broadcasted_iota(jnp.int32, (c, c), 1)
    eye = jnp.where(ii == jj, 1.0, 0.0)
    gout = gout_ref[...]
    units = [(ci, hh) for ci in range(chunks) for hh in range(heads)]
    rows = lambda ci: slice(ci * c, (ci + 1) * c)
    cols = lambda hh: slice(hh * A_DK, (hh + 1) * A_DK)
    q = [q_ref[rows(ci), cols(hh)] for ci, hh in units]
    k = [k_ref[rows(ci), cols(hh)] for ci, hh in units]
    gcol = [gc_ref[rows(ci), hh:hh + 1] for ci, hh in units]
    bcol = [beta_ref[rows(ci), hh:hh + 1] for ci, hh in units]
    kb = [x.astype(DT_KK) for x in k]
    kk = [_dot_nt(x, x) for x in kb]
    qk = [_dot_nt(x.astype(DT_KK), y) for x, y in zip(q, kb)]
    a_kk, a_qk, rhs, eg = [], [], [], []
    for u, (ci, hh) in enumerate(units):
        dec = jnp.exp(jnp.where(ii >= jj, gcol[u] - gct_ref[hh:hh + 1, rows(ci)], -jnp.inf))
        a_kk.append(jnp.where(ii > jj, kk[u] * dec, 0.0) * bcol[u])
        a_qk.append((qk[u] * dec).astype(DT_STATE))
        eg.append(jnp.exp(gcol[u]))
        rhs.append(jnp.concatenate([v_ref[rows(ci), cols(hh)] * bcol[u], k[u] * (bcol[u] * eg[u])], axis=1))
    pw = [(-a).astype(BF16) for a in a_kk]
    t_inv = [eye - a for a in a_kk]
    for _ in range(5):
        pw = [_dot(p, p).astype(BF16) for p in pw]
        t_inv = [t + _dot(t.astype(BF16), p) for t, p in zip(t_inv, pw)]
    tb = [t.astype(BF16) for t in t_inv]
    y0 = [_dot(t, r.astype(BF16)) for t, r in zip(tb, rhs)]
    y_hi = [y.astype(BF16) for y in y0]
    a_hi = [a.astype(BF16) for a in a_kk]
    ay = [_dot(a, jnp.concatenate([yh, (y - yh.astype(F32)).astype(BF16)], axis=1))
          for a, y, yh in zip(a_hi, y0, y_hi)]
    al = [_dot((a - ah.astype(F32)).astype(BF16), yh) for a, ah, yh in zip(a_kk, a_hi, y_hi)]
    resid = [r - y - (p[:, :2 * A_DV] + p[:, 2 * A_DV:] + l) for r, y, p, l in zip(rhs, y0, ay, al)]
    ysol = [y + _dot(t, r.astype(BF16)) for y, t, r in zip(y0, tb, resid)]
    for ci in range(chunks):
        us = [u for u, (cu, _) in enumerate(units) if cu == ci]
        s_old = [s_ref[units[u][1]] for u in us]
        sb = [x.astype(DT_STATE) for x in s_old]
        w_s = [_dot(ysol[u][:, A_DV:].astype(DT_STATE), x) for u, x in zip(us, sb)]
        q_s = [_dot((q[u] * eg[u]).astype(DT_STATE), x) for u, x in zip(us, sb)]
        vnb = [(ysol[u][:, :A_DV] - w).astype(DT_STATE) for u, w in zip(us, w_s)]
        glast = [gc_ref[ci * c + c - 1:ci * c + c, units[u][1]:units[u][1] + 1] for u in us]
        kdec = [(k[u] * jnp.exp(g - gcol[u])).astype(DT_STATE) for u, g in zip(us, glast)]
        o_s = [x + _dot(a_qk[u], vn) for u, x, vn in zip(us, q_s, vnb)]
        s_up = [_dot_tn(kd, vn) for kd, vn in zip(kdec, vnb)]
        for n, u in enumerate(us):
            hh = units[u][1]
            s_ref[hh] = jnp.exp(glast[n]) * s_old[n] + s_up[n]
            o = o_s[n]
            on = o * lax.rsqrt(jnp.mean(o * o, axis=-1, keepdims=True) + EPS) * gout
            og_ref[rows(ci), cols(hh)] = (on * _silu(gate_ref[rows(ci), cols(hh)])).astype(og_ref.dtype)


def _delta(qkv, proj, gc_g, gct_g, beta_g, s0, g_out, layer, *, batch, seq, heads=4, chunks=2):
    m = qkv.shape[0]
    rows = chunks * CHUNK
    nsteps = seq // rows
    w = heads * A_DK
    qo, ko, vo, go = 0, A_DQK // w, 2 * A_DQK // w, QKV_W // w
    blk = lambda off: pl.BlockSpec((rows, w), lambda b, hg, t: (b * nsteps + t, off + hg))
    return pl.pallas_call(
        functools.partial(_delta_body, heads=heads, chunks=chunks),
        grid=(batch, A_HEADS // heads, nsteps),
        in_specs=[
            blk(qo), blk(ko), blk(vo), blk(go),
            pl.BlockSpec((None, rows, heads), lambda b, hg, t: (hg, b * nsteps + t, 0)),
            pl.BlockSpec((None, heads, rows), lambda b, hg, t: (hg, 0, b * nsteps + t)),
            pl.BlockSpec((None, rows, heads), lambda b, hg, t: (hg, b * nsteps + t, 0)),
            pl.BlockSpec((None, heads, A_DK, A_DV), lambda b, hg, t: (b, hg, 0, 0)),
            pl.BlockSpec((None, 1, A_DV), lambda b, hg, t: (layer, 0, 0)),
        ],
        out_specs=[
            pl.BlockSpec((rows, w), lambda b, hg, t: (b * nsteps + t, hg)),
            pl.BlockSpec((None, heads, A_DK, A_DV), lambda b, hg, t: (b, hg, 0, 0)),
        ],
        out_shape=[
            jax.ShapeDtypeStruct((m, A_DVW), BF16),
            jax.ShapeDtypeStruct((batch, A_HEADS, A_DK, A_DV), F32),
        ],
        compiler_params=_params(("parallel", "parallel", "arbitrary")),
        name="delta_rule",
    )(qkv, qkv, qkv, proj, gc_g, gct_g, beta_g, s0, g_out.reshape(N_A, 1, A_DV))


def _lambda(lam_ref, lam_init):
    lv = lam_ref[...]
    return (jnp.exp(jnp.sum(lv[0:1] * lv[1:2], axis=-1, keepdims=True))
            - jnp.exp(jnp.sum(lv[2:3] * lv[3:4], axis=-1, keepdims=True)) + lam_init)


def _attn_body(q_ref, k_ref, v_ref, lam_ref, gsub_ref, o_ref, m_ref, l_ref, acc_ref, *, tq, lam_init):
    qi = pl.program_id(2)
    ki = pl.program_id(3)

    @pl.when(ki == 0)
    def _():
        m_ref[...] = jnp.full(m_ref.shape, -jnp.inf, F32)
        l_ref[...] = jnp.zeros(l_ref.shape, F32)
        acc_ref[...] = jnp.zeros(acc_ref.shape, F32)

    @pl.when(ki <= qi)
    def _():
        q = q_ref[...].astype(BF16)
        k = k_ref[...].astype(BF16)
        v = v_ref[...].astype(BF16)
        row = qi * tq + lax.broadcasted_iota(jnp.int32, (tq, tq), 0)
        col = ki * tq + lax.broadcasted_iota(jnp.int32, (tq, tq), 1)
        mask = col <= row
        for mi in range(2):
            sl = slice(mi * B_DH, (mi + 1) * B_DH)
            s = jnp.where(mask, _dot_nt(q[:, sl], k[:, sl]), -jnp.inf)
            m_old = m_ref[mi]
            m_new = jnp.maximum(m_old, jnp.max(s, axis=-1, keepdims=True))
            alpha = jnp.exp(m_old - m_new)
            p = jnp.exp(s - m_new)
            l_ref[mi] = alpha * l_ref[mi] + jnp.sum(p, axis=-1, keepdims=True)
            acc_ref[mi] = alpha * acc_ref[mi] + _dot(p.astype(BF16), v)
            m_ref[mi] = m_new

    @pl.when(ki == qi)
    def _():
        lam = _lambda(lam_ref, lam_init)
        o = acc_ref[0] / l_ref[0] - lam * (acc_ref[1] / l_ref[1])
        o = o * lax.rsqrt(jnp.mean(o * o, axis=-1, keepdims=True) + EPS) * gsub_ref[...]
        o_ref[...] = (o * (1.0 - lam_init)).astype(o_ref.dtype)


def _attention(q, k, v, lam_b, g_sub, j, lam_init, *, batch, seq, tq=512):
    m = q.shape[0]
    nq = seq // tq
    w = 2 * B_DH
    return pl.pallas_call(
        functools.partial(_attn_body, tq=tq, lam_init=lam_init),
        grid=(batch, B_HEADS, nq, nq),
        in_specs=[
            pl.BlockSpec((tq, w), lambda b, h, qi, ki: (b * nq + qi, h)),
            pl.BlockSpec((tq, w), lambda b, h, qi, ki: (b * nq + jnp.minimum(ki, qi), h)),
            pl.BlockSpec((tq, w), lambda b, h, qi, ki: (b * nq + jnp.minimum(ki, qi), h)),
            pl.BlockSpec((None, 4, B_DH), lambda b, h, qi, ki: (j, 0, 0)),
            pl.BlockSpec((None, 1, B_DV), lambda b, h, qi, ki: (j, 0, 0)),
        ],
        out_specs=pl.BlockSpec((tq, w), lambda b, h, qi, ki: (b * nq + qi, h)),
        out_shape=jax.ShapeDtypeStruct((m, B_HEADS * B_DV), BF16),
        scratch_shapes=[
            pltpu.VMEM((2, tq, 1), F32),
            pltpu.VMEM((2, tq, 1), F32),
            pltpu.VMEM((2, tq, B_DV), F32),
        ],
        compiler_params=_params(("parallel", "parallel", "parallel", "arbitrary")),
        name="diff_attention",
    )(q, k, v, lam_b, g_sub.reshape(-1, 1, B_DV))


QROWS = 16


def _decode_attn_body(pt_ref, q_ref, kn_ref, vn_ref, lam_ref, gsub_ref, *refs, pages, lam_init):
    k_refs = (refs[:pages], refs[pages:2 * pages])
    v_refs = (refs[2 * pages:3 * pages], refs[3 * pages:4 * pages])
    o_ref, qbd_ref, m_ref, l_ref, acc_ref = refs[4 * pages:]
    s = pl.program_id(1)
    rr = lax.broadcasted_iota(jnp.int32, (QROWS, B_DV), 0)
    cc = lax.broadcasted_iota(jnp.int32, (QROWS, B_DV), 1)
    hcols = lambda h: slice(h * B_DV, (h + 1) * B_DV)

    @pl.when(s == 0)
    def _():
        for h in range(B_HEADS):
            qbd_ref[h] = jnp.where(lax.shift_right_logical(cc, 7) == rr, q_ref[:, hcols(h)], 0.0).astype(BF16)
        m_ref[...] = jnp.full(m_ref.shape, -jnp.inf, F32)
        l_ref[...] = jnp.zeros(l_ref.shape, F32)
        acc_ref[...] = jnp.zeros(acc_ref.shape, F32)

    head_rows = lambda h: pl.ds(h, PAGE_SIZE, stride=B_HEADS)
    halves = (slice(0, B_DH), slice(B_DH, 2 * B_DH))
    for p in range(pages):
        sc = [sum(_dot_nt(qbd_ref[h, :, halves[x]], k_refs[x][p][head_rows(h), :].astype(BF16)) for x in (0, 1))
              for h in range(B_HEADS)]
        for h in range(B_HEADS):
            m_old = m_ref[h]
            m_new = jnp.maximum(m_old, jnp.max(sc[h], axis=-1, keepdims=True))
            alpha = jnp.exp(m_old - m_new)
            pr = jnp.exp(sc[h] - m_new)
            l_ref[h] = alpha * l_ref[h] + jnp.sum(pr, axis=-1, keepdims=True)
            pb = pr.astype(BF16)
            for x in (0, 1):
                acc_ref[h, :, halves[x]] = (alpha * acc_ref[h, :, halves[x]]
                                            + _dot(pb, v_refs[x][p][head_rows(h), :].astype(BF16)))
            m_ref[h] = m_new

    @pl.when(s == pl.num_programs(1) - 1)
    def _():
        lam = _lambda(lam_ref, lam_init)
        for h in range(B_HEADS):
            kn = kn_ref[:, hcols(h)].astype(BF16).astype(F32)
            vn = vn_ref[:, hcols(h)].astype(BF16).astype(F32)
            s_new = jnp.sum(qbd_ref[h].astype(F32) * kn, axis=-1, keepdims=True)
            m_old = m_ref[h]
            m_new = jnp.maximum(m_old, s_new)
            alpha = jnp.exp(m_old - m_new)
            pn = jnp.exp(s_new - m_new)
            on = (alpha * acc_ref[h] + pn.astype(BF16).astype(F32) * vn) / (alpha * l_ref[h] + pn)
            o = on[0:1] - lam * on[1:2]
            o = o * lax.rsqrt(jnp.mean(o * o, axis=-1, keepdims=True) + EPS) * gsub_ref[...]
            o_ref[:, hcols(h)] = o * (1.0 - lam_init)


def _decode_attention(q, k_new, v_new, cache_k, cache_v, page_table, lam_b, g_sub, j, lam_init, *, pages=4):
    bsz, width = q.shape
    n_pages = page_table.shape[1]
    nsteps = n_pages // pages
    row3 = lambda a: a.reshape(bsz, 1, width)
    rspec = pl.BlockSpec((None, 1, width), lambda b, s, pt: (b, 0, 0))

    def page_spec(p, half):
        return pl.BlockSpec((None, PAGE_SIZE * B_HEADS, B_DH),
                            lambda b, s, pt: (pt[b * n_pages + s * pages + p], 0, half))

    page_specs = [page_spec(p, half) for half in (0, 1) for p in range(pages)]

    grid_spec = pltpu.PrefetchScalarGridSpec(
        num_scalar_prefetch=1,
        grid=(bsz, nsteps),
        in_specs=[
            rspec, rspec, rspec,
            pl.BlockSpec((None, 4, B_DH), lambda b, s, pt: (j, 0, 0)),
            pl.BlockSpec((None, 1, B_DV), lambda b, s, pt: (j, 0, 0)),
        ] + page_specs * 2,
        out_specs=rspec,
        scratch_shapes=[
            pltpu.VMEM((B_HEADS, QROWS, B_DV), BF16),
            pltpu.VMEM((B_HEADS, QROWS, 1), F32),
            pltpu.VMEM((B_HEADS, QROWS, 1), F32),
            pltpu.VMEM((B_HEADS, QROWS, B_DV), F32),
        ],
    )
    out = pl.pallas_call(
        functools.partial(_decode_attn_body, pages=pages, lam_init=lam_init),
        grid_spec=grid_spec,
        out_shape=jax.ShapeDtypeStruct((bsz, 1, width), F32),
        compiler_params=_params(("parallel", "arbitrary")),
        name="decode_attention",
    )(page_table.reshape(-1), row3(q), row3(k_new), row3(v_new), lam_b, g_sub.reshape(-1, 1, B_DV),
      *([cache_k] * (2 * pages)), *([cache_v] * (2 * pages)))
    return out.reshape(bsz, width)


def _lambda_init(layer):
    return 0.8 - 0.6 * math.exp(-0.3 * layer)


def _rope_tables(pos):
    half = B_DH // 2
    inv = ROPE_THETA ** (-jnp.arange(half, dtype=F32) / half)
    ang = pos.astype(F32)[:, None] * inv
    cos, sin = jnp.cos(ang), jnp.sin(ang)
    return jnp.concatenate([cos, cos], axis=-1), jnp.concatenate([-sin, sin], axis=-1)


def _group_heads(a, heads):
    m = a.shape[0]
    return a.reshape(m, A_HEADS // heads, heads).transpose(1, 0, 2)


def _trunk(x3, mods, mod_kv, weights, *, prompt, s_a, qkv_bufs, ffn_bufs, past):
    (g_mix, g_ffn, w_in_a, conv_a, a_log, dt_bias, g_out_a, w_out_a, g_kv, w_k, w_v, w_q_b, lam_b, g_sub_b,
     w_o_b, w_gate, w_up, conv_f, b_conv_f, w_down, g_final) = weights
    bsz, t, d = x3.shape
    heads = min(8, A_HEADS)
    if prompt:
        nb, nt = bsz, t
        tm, tm_n, tm_dn = min(1024, t), min(512, t), min(512, t)
        mod3 = lambda mm: mm.reshape(bsz, 1, -1)
        pos = jnp.arange(t, dtype=jnp.int32)
    else:
        nb, nt = 1, bsz
        tm = tm_n = tm_dn = bsz
        mod3 = lambda mm: mm.reshape(1, bsz, -1)
        pos = jnp.full((bsz,), past[3], dtype=jnp.int32)
    m = nb * nt
    x = x3.reshape(m, d)
    rope = _rope_tables(pos)
    new_s, new_qkv, new_ffn = [], [], []
    k_new = v_new = None
    for l in range(DEPTH):
        md = mod3(mods[l])
        h = _norm_mod(x.reshape(nb, nt, d), g_mix, l, md, 0, 1, tm_n, BF16 if prompt else F32).reshape(m, d)
        if l < N_A:
            proj = _matmul(h, w_in_a, layer=l, n_out=QKV_W + A_DVW, tm=tm, tn=512, rows_per_seq=nt, name="w_in")
            w_ab = lax.slice(w_in_a, (l, 0, QKV_W + A_DVW), (l + 1, d, QKV_W + A_DVW + 2 * A_HEADS))[0]
            w_ab = jnp.pad(w_ab, ((0, 0), (0, LANES - 2 * A_HEADS)))
            ab = _matmul(h, w_ab, n_out=LANES, tm=tm, tn=LANES, rows_per_seq=nt, name="w_in_ab")
            if prompt:
                prev = jnp.pad(qkv_bufs[l], ((0, 0), (8 - (CONV_W - 1), 0), (0, 0)))
                new_qkv.append(proj.reshape(bsz, t, -1)[:, t - (CONV_W - 1):, :QKV_W])
            else:
                prev = jnp.swapaxes(qkv_bufs[l], 0, 1)
                new_qkv.append(jnp.concatenate([qkv_bufs[l][:, 1:], proj[:, None, :QKV_W]], axis=1))
            qkv = _qkvconv(proj, conv_a, l, prev, tm=min(512, nt), tn=512, rows_per_seq=nt, seq_rows=prompt)
            g_full, beta_full = _gates(ab, a_log[l], dt_bias[l], tm=min(512, m), cumsum=prompt)
            gc = g_full[:, :A_HEADS]
            beta = beta_full[:, A_HEADS:2 * A_HEADS]
            if prompt:
                qkv_d, proj_d, dseq = qkv, proj, t
            else:
                dseq = 2 * CHUNK
                padrows = lambda a: jnp.pad(a[:, None], ((0, 0), (0, dseq - 1), (0, 0))).reshape(bsz * dseq, -1)
                qkv_d, proj_d, beta = padrows(qkv), padrows(proj), padrows(beta)
                gc = jnp.concatenate([jnp.broadcast_to(gc[:, None], (bsz, CHUNK, A_HEADS)),
                                      jnp.zeros((bsz, CHUNK, A_HEADS), F32)], axis=1).reshape(bsz * dseq, -1)
            og, s_fin = _delta(qkv_d, proj_d, _group_heads(gc, heads), jnp.swapaxes(_group_heads(gc, heads), 1, 2),
                               _group_heads(beta, heads), s_a[l], g_out_a, l, batch=bsz, seq=dseq, heads=heads)
            if not prompt:
                og = og.reshape(bsz, dseq, -1)[:, 0].astype(F32)
            new_s.append(s_fin)
            x = _matmul(og, w_out_a, layer=l, n_out=d, tm=tm, tn=512, rows_per_seq=nt, epi="resid", res=x,
                        mod3=md, k_gate=2, name="w_out")
        else:
            j = l - N_A
            lam_init = _lambda_init(l)
            q = _matmul(h, w_q_b, layer=j, n_out=d, tm=tm, tn=512, rows_per_seq=nt, epi="rope", rope=rope,
                        scale=B_DH ** -0.5, out_dtype=BF16 if prompt else F32, name="w_q")
            if prompt:
                o = _attention(q, k_new, v_new, lam_b, g_sub_b, j, lam_init, batch=bsz, seq=t, tq=min(512, t))
            else:
                o = _decode_attention(q, k_new, v_new, past[0], past[1], past[2], lam_b, g_sub_b, j, lam_init)
            x = _matmul(o, w_o_b, layer=j, n_out=d, tm=tm, tn=512, rows_per_seq=nt, epi="resid", res=x,
                        mod3=md, k_gate=2, name="w_o")
        h = _norm_mod(x.reshape(nb, nt, d), g_ffn, l, md, 3, 4, tm_n, BF16 if prompt else F32).reshape(m, d)
        if prompt:
            prev = jnp.pad(ffn_bufs[l], ((0, 0), (16 - (FFN_CONV_W - 1), 0), (0, 0)))
        else:
            prev = jnp.swapaxes(ffn_bufs[l], 0, 1)
        a, nbuf = _ffn1(h, w_gate, w_up, conv_f, b_conv_f, l, prev, tm=tm, tn=256, rows_per_seq=nt, seq_rows=prompt)
        if prompt:
            new_ffn.append(nbuf[nt // tm - 1::nt // tm, 8 - (FFN_CONV_W - 1):])
        else:
            new_ffn.append(jnp.stack([ffn_bufs[l][:, 1], nbuf[0]], axis=1))
        x = _matmul(a, w_down, layer=l, n_out=d, tm=tm_dn, tn=256, rows_per_seq=nt, epi="resid", res=x, mod3=md,
                    k_gate=5, name="w_down")
        if l == N_A - 1:
            hkv = _norm_mod(x.reshape(nb, nt, d), g_kv, 0, mod3(mod_kv), 0, 1, tm_n, BF16 if prompt else F32)
            hkv = hkv.reshape(m, d)
            k_new = _matmul(hkv, w_k, n_out=d, tm=tm, tn=512, rows_per_seq=nt, epi="rope", rope=rope, name="w_k")
            v_new = _matmul(hkv, w_v, n_out=d, tm=tm, tn=512, rows_per_seq=nt, name="w_v")
    y = _final_norm(x.reshape(nb, nt, d), g_final, tm_n).reshape(bsz, t, d)
    return (y, k_new.reshape(bsz, t, B_HEADS, 2 * B_DH), v_new.reshape(bsz, t, B_HEADS, B_DV),
            jnp.stack(new_s), jnp.stack(new_qkv), jnp.stack(new_ffn))


def kernel(x_prompt, x_sample, cache_k, cache_v, state_delta, state_qkv_conv, state_ffn_conv, page_table, c_prompt, c_sample, w_ada, b_ada, g_mix, g_ffn, w_in_a, conv_a, a_log, dt_bias, g_out_a, w_out_a, w_ada_kv, b_ada_kv, g_kv, w_k, w_v, w_q_b, lam_b, g_sub_b, w_o_b, w_gate, w_up, conv_f, b_conv_f, w_down, g_final):
    bp, _, d = x_prompt.shape
    bs = x_sample.shape[0]
    c_all = jnp.concatenate([jnp.pad(c_prompt, ((0, 8 - bp), (0, 0))), c_sample], axis=0)
    mods = _mod_all(c_all, w_ada, b_ada)
    mod_kv = _mod_all(c_all, w_ada_kv[None], b_ada_kv[None])[0]
    weights = (g_mix, g_ffn, w_in_a, conv_a, a_log, dt_bias, g_out_a, w_out_a, g_kv, w_k, w_v, w_q_b, lam_b,
               g_sub_b, w_o_b, w_gate, w_up, conv_f, b_conv_f, w_down, g_final)
    dt = x_prompt.dtype
    yp, kp, vp, sp, qcp, fcp = _trunk(
        x_prompt, mods[:, :bp], mod_kv[:bp], weights, prompt=True,
        s_a=jnp.zeros((N_A, bp, A_HEADS, A_DK, A_DV), state_delta.dtype),
        qkv_bufs=jnp.zeros((N_A, bp, CONV_W - 1, QKV_W), dt),
        ffn_bufs=jnp.zeros((DEPTH, bp, FFN_CONV_W - 1, D_FF), dt), past=None)
    n_pool = cache_k.shape[0]
    past = (cache_k.reshape(n_pool, PAGE_SIZE * B_HEADS, -1), cache_v.reshape(n_pool, PAGE_SIZE * B_HEADS, -1),
            page_table, page_table.shape[1] * PAGE_SIZE)
    ys, ks, vs, ss, qcs, fcs = _trunk(
        x_sample, mods[:, 8:8 + bs], mod_kv[8:8 + bs], weights, prompt=False,
        s_a=state_delta, qkv_bufs=state_qkv_conv, ffn_bufs=state_ffn_conv, past=past)
    return (yp, ys, kp, vp, ks, vs, sp, ss, qcp, qcs, fcp, fcs)
```

```python
import functools
import math

import jax
import jax.numpy as jnp
from jax import lax
from jax.experimental import pallas as pl
from jax.experimental.pallas import tpu as pltpu

F32 = jnp.float32
BF16 = jnp.bfloat16
DT_KK = BF16
DT_STATE = BF16

D_MODEL = 4096
DEPTH = 4
N_A = DEPTH // 2
A_HEADS = 32
A_DK = 128
A_DV = 128
A_DQK = A_HEADS * A_DK
A_DVW = A_HEADS * A_DV
QKV_W = 2 * A_DQK + A_DVW
CONV_W = 4
CHUNK = 64
B_HEADS = 16
B_DH = 128
B_DV = 256
ROPE_THETA = 10000.0
D_FF = 11008
FFN_CONV_W = 3
EPS = 1e-6
PAGE_SIZE = 128

VMEM_LIMIT_BYTES = 60000 * 1024
LANES = 128
FFN_SUB_ROWS = 1024


def _params(sem):
    return pltpu.CompilerParams(dimension_semantics=sem, vmem_limit_bytes=VMEM_LIMIT_BYTES)


def _dot(a, b):
    return jnp.dot(a, b, preferred_element_type=F32)


def _dot_f32(a, b):
    return jnp.dot(a, b, preferred_element_type=F32, precision=lax.Precision.HIGHEST)


def _dot_nt(a, b):
    return lax.dot_general(a, b, (((1,), (1,)), ((), ())), preferred_element_type=F32)


def _dot_tn(a, b):
    return lax.dot_general(a, b, (((0,), (0,)), ((), ())), preferred_element_type=F32)


def _silu(x):
    return x * jax.nn.sigmoid(x)


def _panel_spec(tm, k):
    return pl.BlockSpec((tm, k), lambda i, j: (i, 0), pipeline_mode=pl.Buffered(1))


def _wspec(w, layer, k, tn, off_blocks):
    if w.ndim == 3:
        return pl.BlockSpec((None, k, tn), lambda i, j: (layer, 0, j + off_blocks))
    return pl.BlockSpec((k, tn), lambda i, j: (0, j + off_blocks))


def _mod_body(c_ref, w_ref, b_ref, o_ref):
    c = c_ref[...]
    o_ref[...] = _dot(_silu(c).astype(BF16), w_ref[...].astype(BF16)) + b_ref[...]


def _mod_all(c_all, w, b, tn=512):
    nl, k, n = w.shape
    mp = c_all.shape[0]
    return pl.pallas_call(
        _mod_body,
        grid=(nl, n // tn),
        in_specs=[
            pl.BlockSpec((mp, k), lambda l, j: (0, 0)),
            pl.BlockSpec((None, k, tn), lambda l, j: (l, 0, j)),
            pl.BlockSpec((None, 1, tn), lambda l, j: (l, 0, j)),
        ],
        out_specs=pl.BlockSpec((None, mp, tn), lambda l, j: (l, 0, j)),
        out_shape=jax.ShapeDtypeStruct((nl, mp, n), F32),
        compiler_params=_params(("parallel", "parallel")),
        name="adaln_mod",
    )(c_all, w, b.reshape(nl, 1, n))


def _norm_mod_body(x_ref, g_ref, sh_ref, sc_ref, o_ref):
    x = x_ref[...]
    y = x * lax.rsqrt(jnp.mean(x * x, axis=-1, keepdims=True) + EPS) * g_ref[...]
    o_ref[...] = (y * (1.0 + sc_ref[...]) + sh_ref[...]).astype(o_ref.dtype)


def _norm_body(x_ref, g_ref, o_ref):
    x = x_ref[...]
    o_ref[...] = (x * lax.rsqrt(jnp.mean(x * x, axis=-1, keepdims=True) + EPS) * g_ref[...]).astype(o_ref.dtype)


def _norm_mod(x3, g, layer, mod3, k_shift, k_scale, tm, out_dtype):
    b, t, d = x3.shape
    g3 = g.reshape(-1, 1, d)
    r = mod3.shape[1]
    rb = 1 if r == 1 else tm
    mspec = lambda k: pl.BlockSpec((None, rb, d), lambda bi, i: (bi, 0 if r == 1 else i, k))
    return pl.pallas_call(
        _norm_mod_body,
        grid=(b, t // tm),
        in_specs=[
            pl.BlockSpec((None, tm, d), lambda bi, i: (bi, i, 0)),
            pl.BlockSpec((None, 1, d), lambda bi, i: (layer, 0, 0)),
            mspec(k_shift),
            mspec(k_scale),
        ],
        out_specs=pl.BlockSpec((None, tm, d), lambda bi, i: (bi, i, 0)),
        out_shape=jax.ShapeDtypeStruct((b, t, d), out_dtype),
        compiler_params=_params(("parallel", "parallel")),
        name="norm_mod",
    )(x3, g3, mod3, mod3)


def _final_norm(x3, g, tm):
    b, t, d = x3.shape
    return pl.pallas_call(
        _norm_body,
        grid=(b, t // tm),
        in_specs=[
            pl.BlockSpec((None, tm, d), lambda bi, i: (bi, i, 0)),
            pl.BlockSpec((1, d), lambda bi, i: (0, 0)),
        ],
        out_specs=pl.BlockSpec((None, tm, d), lambda bi, i: (bi, i, 0)),
        out_shape=jax.ShapeDtypeStruct((b, t, d), F32),
        compiler_params=_params(("parallel", "parallel")),
        name="final_norm",
    )(x3, g.reshape(1, d))


def _mm_body(x_ref, w_ref, *rest, epi, scale, n_outs):
    o_refs = rest[len(rest) - n_outs:]
    acc = _dot(x_ref[...].astype(BF16), w_ref[...].astype(BF16))
    if epi == "plain":
        out = acc
    elif epi == "resid":
        res_ref, gt_ref = rest[:2]
        out = res_ref[...] + gt_ref[...] * acc
    else:
        cs_ref, sn_ref = rest[:2]
        cs = cs_ref[...]
        sn = sn_ref[...]
        parts = []
        for gi in range(acc.shape[1] // LANES):
            a = acc[:, gi * LANES:(gi + 1) * LANES]
            parts.append(a * cs + pltpu.roll(a, LANES // 2, axis=1) * sn)
        out = jnp.concatenate(parts, axis=1)
        if scale != 1.0:
            out = out * scale
    for o_ref in o_refs:
        o_ref[...] = out.astype(o_ref.dtype)


def _matmul(x, w, *, layer=0, col_off=0, n_out, tm, tn, rows_per_seq, epi="plain", res=None, mod3=None,
            k_gate=0, rope=None, scale=1.0, out_dtype=F32, name="matmul"):
    m, k = x.shape
    tps = rows_per_seq // tm
    in_specs = [_panel_spec(tm, k), _wspec(w, layer, k, tn, col_off // tn)]
    args = [x, w]
    if epi == "resid":
        r = mod3.shape[1]
        rb = 1 if r == 1 else tm
        goff = k_gate * n_out // tn
        in_specs += [
            pl.BlockSpec((tm, tn), lambda i, j: (i, j)),
            pl.BlockSpec((None, rb, tn), lambda i, j: (i // tps, 0 if r == 1 else i % tps, goff + j)),
        ]
        args += [res, mod3]
    elif epi == "rope":
        in_specs += [pl.BlockSpec((tm, LANES), lambda i, j: (i % tps, 0))] * 2
        args += list(rope)
    dtypes = out_dtype if isinstance(out_dtype, tuple) else (out_dtype,)
    outs = pl.pallas_call(
        functools.partial(_mm_body, epi=epi, scale=scale, n_outs=len(dtypes)),
        grid=(m // tm, n_out // tn),
        in_specs=in_specs,
        out_specs=[pl.BlockSpec((tm, tn), lambda i, j: (i, j))] * len(dtypes),
        out_shape=[jax.ShapeDtypeStruct((m, n_out), dt) for dt in dtypes],
        compiler_params=_params(("parallel", "parallel")),
        name=name,
    )(*args)
    return outs if isinstance(out_dtype, tuple) else outs[0]


def _ffn1_body(x_ref, wg_ref, wu_ref, cw_ref, cb_ref, p_ref, a_ref, nb_ref, *scratch, tps, seq_rows):
    i = pl.program_id(0)
    j = pl.program_id(1)
    wg = wg_ref[...].astype(BF16)
    wu = wu_ref[...].astype(BF16)
    cw = cw_ref[...]
    cb = cb_ref[...]
    tm, tn = a_ref.shape
    if not seq_rows:
        x = x_ref[...].astype(BF16)
        g = _dot(x, wg)
        nb_ref[...] = g
        y = cw[2:3] * g + cw[1:2] * p_ref[1] + cw[0:1] * p_ref[0] + cb
        a_ref[...] = (_silu(y) * _dot(x, wu)).astype(a_ref.dtype)
        return
    (carry_ref,) = scratch

    @pl.when(i == 0)
    def _():
        carry_ref[j] = jnp.zeros((16, tn), F32)

    prev = jnp.where((i % tps) == 0, p_ref[...], carry_ref[j])
    r16 = lax.broadcasted_iota(jnp.int32, (16, tn), 0)
    sub = min(tm, FFN_SUB_ROWS)
    for s in range(tm // sub):
        rs = slice(s * sub, (s + 1) * sub)
        x = x_ref[rs, :].astype(BF16)
        g = _dot(x, wg)
        u = _dot(x, wu)
        head = g[0:16]

        def shifted(k):
            h16 = jnp.where(r16 < k, pltpu.roll(prev, k, axis=0), pltpu.roll(head, k, axis=0))
            return jnp.concatenate([h16, pltpu.roll(g, k, axis=0)[16:]], axis=0)

        y = cw[2:3] * g + cw[1:2] * shifted(1) + cw[0:1] * shifted(2) + cb
        a_ref[rs, :] = (_silu(y) * u).astype(a_ref.dtype)
        prev = g[sub - 16:sub]
    carry_ref[j] = prev
    nb_ref[...] = prev[8:16]


def _ffn1(h, w_gate, w_up, conv_f, b_conv_f, layer, prev, *, tm, tn, rows_per_seq, seq_rows):
    m, k = h.shape
    n = D_FF
    tps = rows_per_seq // tm
    nb_rows = 8 if seq_rows else m
    if seq_rows:
        pspec = pl.BlockSpec((None, 16, tn), lambda i, j: (i // tps, 0, j))
        scratch = [pltpu.VMEM((n // tn, 16, tn), F32)]
    else:
        pspec = pl.BlockSpec((2, m, tn), lambda i, j: (0, 0, j))
        scratch = []
    return pl.pallas_call(
        functools.partial(_ffn1_body, tps=tps, seq_rows=seq_rows),
        grid=(m // tm, n // tn),
        in_specs=[
            _panel_spec(tm, k),
            pl.BlockSpec((None, k, tn), lambda i, j: (layer, 0, j)),
            pl.BlockSpec((None, k, tn), lambda i, j: (layer, 0, j)),
            pl.BlockSpec((None, FFN_CONV_W, tn), lambda i, j: (layer, 0, j)),
            pl.BlockSpec((None, 1, tn), lambda i, j: (layer, 0, j)),
            pspec,
        ],
        out_specs=[
            pl.BlockSpec((tm, tn), lambda i, j: (i, j)),
            pl.BlockSpec((None, nb_rows, tn), lambda i, j: (i, 0, j)),
        ],
        out_shape=[
            jax.ShapeDtypeStruct((m, n), BF16 if seq_rows else F32),
            jax.ShapeDtypeStruct((m // tm, nb_rows, n), F32),
        ],
        scratch_shapes=scratch,
        compiler_params=_params(("arbitrary", "arbitrary")),
        name="ffn_gate_up",
    )(h, w_gate, w_up, conv_f, b_conv_f.reshape(DEPTH, 1, n), prev)


def _qkvconv_body(x_ref, cw_ref, p_ref, o_ref, *scratch, tps, seq_rows, q_blocks, qk_blocks):
    jn = pl.program_id(0)
    i = pl.program_id(1)
    x = x_ref[...]
    cw = cw_ref[...]
    tm, tn = x.shape
    if seq_rows:
        (carry_ref,) = scratch

        @pl.when(i == 0)
        def _():
            carry_ref[...] = jnp.zeros((8, tn), F32)

        prev = jnp.where((i % tps) == 0, p_ref[...], carry_ref[...])
        r8 = lax.broadcasted_iota(jnp.int32, (8, tn), 0)
        head = x[0:8]

        def shifted(s):
            h8 = jnp.where(r8 < s, pltpu.roll(prev, s, axis=0), pltpu.roll(head, s, axis=0))
            return jnp.concatenate([h8, pltpu.roll(x, s, axis=0)[8:]], axis=0)

        x1, x2, x3 = shifted(1), shifted(2), shifted(3)
        carry_ref[...] = x[tm - 8:tm]
    else:
        x1, x2, x3 = p_ref[2], p_ref[1], p_ref[0]
    y = _silu(cw[3:4] * x + cw[2:3] * x1 + cw[1:2] * x2 + cw[0:1] * x3)
    mult = jnp.where(jn < q_blocks, A_DK ** -0.5, 1.0)
    is_qk = jn < qk_blocks
    parts = []
    for gi in range(tn // LANES):
        z = y[:, gi * LANES:(gi + 1) * LANES]
        nrm = lax.rsqrt(jnp.sum(z * z, axis=-1, keepdims=True) + EPS) * mult
        parts.append(z * jnp.where(is_qk, nrm, 1.0))
    o_ref[...] = jnp.concatenate(parts, axis=1)


def _qkvconv(proj, conv_a, layer, prev, *, tm, tn, rows_per_seq, seq_rows):
    m = proj.shape[0]
    tps = rows_per_seq // tm
    if seq_rows:
        pspec = pl.BlockSpec((None, 8, tn), lambda jn, i: (i // tps, 0, jn))
        scratch = [pltpu.VMEM((8, tn), F32)]
    else:
        pspec = pl.BlockSpec((CONV_W - 1, m, tn), lambda jn, i: (0, 0, jn))
        scratch = []
    return pl.pallas_call(
        functools.partial(_qkvconv_body, tps=tps, seq_rows=seq_rows, q_blocks=A_DQK // tn,
                          qk_blocks=2 * A_DQK // tn),
        grid=(QKV_W // tn, m // tm),
        in_specs=[
            pl.BlockSpec((tm, tn), lambda jn, i: (i, jn)),
            pl.BlockSpec((None, CONV_W, tn), lambda jn, i: (layer, 0, jn)),
            pspec,
        ],
        out_specs=pl.BlockSpec((tm, tn), lambda jn, i: (i, jn)),
        out_shape=jax.ShapeDtypeStruct((m, QKV_W), F32),
        scratch_shapes=scratch,
        compiler_params=_params(("arbitrary", "arbitrary")),
        name="qkv_conv",
    )(proj, conv_a, prev)


def _gates_body(ab_ref, alog_ref, dtb_ref, g_ref, beta_ref, *, cumsum):
    ab = ab_ref[...]
    x = ab + dtb_ref[...]
    softplus = jnp.maximum(x, 0.0) + jnp.log1p(jnp.exp(-jnp.abs(x)))
    g = -jnp.exp(alog_ref[...]) * softplus
    if cumsum:
        r = lax.broadcasted_iota(jnp.int32, g.shape, 0) & (CHUNK - 1)
        s = 1
        while s < CHUNK:
            g = g + jnp.where(r >= s, pltpu.roll(g, s, axis=0), 0.0)
            s *= 2
    g_ref[...] = g
    beta_ref[...] = jax.nn.sigmoid(ab)


def _gates(ab, a_log, dt_bias, *, tm, cumsum):
    m = ab.shape[0]
    pad = lambda v: jnp.pad(v.astype(F32), (0, LANES - A_HEADS)).reshape(1, LANES)
    return pl.pallas_call(
        functools.partial(_gates_body, cumsum=cumsum),
        grid=(m // tm,),
        in_specs=[
            pl.BlockSpec((tm, LANES), lambda i: (i, 0)),
            pl.BlockSpec((1, LANES), lambda i: (0, 0)),
            pl.BlockSpec((1, LANES), lambda i: (0, 0)),
        ],
        out_specs=[pl.BlockSpec((tm, LANES), lambda i: (i, 0))] * 2,
        out_shape=[jax.ShapeDtypeStruct((m, LANES), F32)] * 2,
        compiler_params=_params(("parallel",)),
        name="delta_gates",
    )(ab, pad(a_log), pad(dt_bias))


def _delta_body(q_ref, k_ref, v_ref, gate_ref, gc_ref, gct_ref, beta_ref, s0_ref, gout_ref, og_ref, s_ref, *,
                heads, chunks):
    @pl.when(pl.program_id(2) == 0)
    def _():
        s_ref[...] = s0_ref[...]

    c = CHUNK
    ii = lax.broadcasted_iota(jnp.int32, (c, c), 0)
    jj = lax.broadcasted_iota(jnp.int32, (c, c), 1)
    eye = jnp.where(ii == jj, 1.0, 0.0)
    gout = gout_ref[...]
    units = [(ci, hh) for ci in range(chunks) for hh in range(heads)]
    rows = lambda ci: slice(ci * c, (ci + 1) * c)
    cols = lambda hh: slice(hh * A_DK, (hh + 1) * A_DK)
    q = [q_ref[rows(ci), cols(hh)] for ci, hh in units]
    k = [k_ref[rows(ci), cols(hh)] for ci, hh in units]
    gcol = [gc_ref[rows(ci), hh:hh + 1] for ci, hh in units]
    bcol = [beta_ref[rows(ci), hh:hh + 1] for ci, hh in units]
    kb = [x.astype(DT_KK) for x in k]
    kk = [_dot_nt(x, x) for x in kb]
    qk = [_dot_nt(x.astype(DT_KK), y) for x, y in zip(q, kb)]
    a_kk, a_qk, rhs, eg = [], [], [], []
    for u, (ci, hh) in enumerate(units):
        dec = jnp.exp(jnp.where(ii >= jj, gcol[u] - gct_ref[hh:hh + 1, rows(ci)], -jnp.inf))
        a_kk.append(jnp.where(ii > jj, kk[u] * dec, 0.0) * bcol[u])
        a_qk.append((qk[u] * dec).astype(DT_STATE))
        eg.append(jnp.exp(gcol[u]))
        rhs.append(jnp.concatenate([v_ref[rows(ci), cols(hh)] * bcol[u], k[u] * (bcol[u] * eg[u])], axis=1))
    pw = [(-a).astype(BF16) for a in a_kk]
    t_inv = [eye - a for a in a_kk]
    for _ in range(5):
        pw = [_dot(p, p).astype(BF16) for p in pw]
        t_inv = [t + _dot(t.astype(BF16), p) for t, p in zip(t_inv, pw)]
    tb = [t.astype(BF16) for t in t_inv]
    y0 = [_dot(t, r.astype(BF16)) for t, r in zip(tb, rhs)]
    y_hi = [y.astype(BF16) for y in y0]
    a_hi = [a.astype(BF16) for a in a_kk]
    ay = [_dot(a, jnp.concatenate([yh, (y - yh.astype(F32)).astype(BF16)], axis=1))
          for a, y, yh in zip(a_hi, y0, y_hi)]
    al = [_dot((a - ah.astype(F32)).astype(BF16), yh) for a, ah, yh in zip(a_kk, a_hi, y_hi)]
    resid = [r - y - (p[:, :2 * A_DV] + p[:, 2 * A_DV:] + l) for r, y, p, l in zip(rhs, y0, ay, al)]
    ysol = [y + _dot(t, r.astype(BF16)) for y, t, r in zip(y0, tb, resid)]
    for ci in range(chunks):
        us = [u for u, (cu, _) in enumerate(units) if cu == ci]
        s_old = [s_ref[units[u][1]] for u in us]
        sb = [x.astype(DT_STATE) for x in s_old]
        w_s = [_dot(ysol[u][:, A_DV:].astype(DT_STATE), x) for u, x in zip(us, sb)]
        q_s = [_dot((q[u] * eg[u]).astype(DT_STATE), x) for u, x in zip(us, sb)]
        vnb = [(ysol[u][:, :A_DV] - w).astype(DT_STATE) for u, w in zip(us, w_s)]
        glast = [gc_ref[ci * c + c - 1:ci * c + c, units[u][1]:units[u][1] + 1] for u in us]
        kdec = [(k[u] * jnp.exp(g - gcol[u])).astype(DT_STATE) for u, g in zip(us, glast)]
        o_s = [x + _dot(a_qk[u], vn) for u, x, vn in zip(us, q_s, vnb)]
        s_up = [_dot_tn(kd, vn) for kd, vn in zip(kdec, vnb)]
        for n, u in enumerate(us):
            hh = units[u][1]
            s_ref[hh] = jnp.exp(glast[n]) * s_old[n] + s_up[n]
            o = o_s[n]
            on = o * lax.rsqrt(jnp.mean(o * o, axis=-1, keepdims=True) + EPS) * gout
            og_ref[rows(ci), cols(hh)] = (on * _silu(gate_ref[rows(ci), cols(hh)])).astype(og_ref.dtype)


def _delta(qkv, proj, gc_g, gct_g, beta_g, s0, g_out, layer, *, batch, seq, heads=4, chunks=2):
    m = qkv.shape[0]
    rows = chunks * CHUNK
    nsteps = seq // rows
    w = heads * A_DK
    qo, ko, vo, go = 0, A_DQK // w, 2 * A_DQK // w, QKV_W // w
    blk = lambda off: pl.BlockSpec((rows, w), lambda b, hg, t: (b * nsteps + t, off + hg))
    return pl.pallas_call(
        functools.partial(_delta_body, heads=heads, chunks=chunks),
        grid=(batch, A_HEADS // heads, nsteps),
        in_specs=[
            blk(qo), blk(ko), blk(vo), blk(go),
            pl.BlockSpec((None, rows, heads), lambda b, hg, t: (hg, b * nsteps + t, 0)),
            pl.BlockSpec((None, heads, rows), lambda b, hg, t: (hg, 0, b * nsteps + t)),
            pl.BlockSpec((None, rows, heads), lambda b, hg, t: (hg, b * nsteps + t, 0)),
            pl.BlockSpec((None, heads, A_DK, A_DV), lambda b, hg, t: (b, hg, 0, 0)),
            pl.BlockSpec((None, 1, A_DV), lambda b, hg, t: (layer, 0, 0)),
        ],
        out_specs=[
            pl.BlockSpec((rows, w), lambda b, hg, t: (b * nsteps + t, hg)),
            pl.BlockSpec((None, heads, A_DK, A_DV), lambda b, hg, t: (b, hg, 0, 0)),
        ],
        out_shape=[
            jax.ShapeDtypeStruct((m, A_DVW), BF16),
            jax.ShapeDtypeStruct((batch, A_HEADS, A_DK, A_DV), F32),
        ],
        compiler_params=_params(("parallel", "parallel", "arbitrary")),
        name="delta_rule",
    )(qkv, qkv, qkv, proj, gc_g, gct_g, beta_g, s0, g_out.reshape(N_A, 1, A_DV))


def _lambda(lam_ref, lam_init):
    lv = lam_ref[...]
    return (jnp.exp(jnp.sum(lv[0:1] * lv[1:2], axis=-1, keepdims=True))
            - jnp.exp(jnp.sum(lv[2:3] * lv[3:4], axis=-1, keepdims=True)) + lam_init)


def _attn_body(q_ref, k_ref, v_ref, lam_ref, gsub_ref, o_ref, m_ref, l_ref, acc_ref, *, tq, lam_init):
    qi = pl.program_id(2)
    ki = pl.program_id(3)

    @pl.when(ki == 0)
    def _():
        m_ref[...] = jnp.full(m_ref.shape, -jnp.inf, F32)
        l_ref[...] = jnp.zeros(l_ref.shape, F32)
        acc_ref[...] = jnp.zeros(acc_ref.shape, F32)

    def step(diagonal):
        q = q_ref[...].astype(BF16)
        k = k_ref[...].astype(BF16)
        v = v_ref[...].astype(BF16)
        if diagonal:
            mask = (lax.broadcasted_iota(jnp.int32, (tq, tq), 1) <= lax.broadcasted_iota(jnp.int32, (tq, tq), 0))
        for mi in range(2):
            sl = slice(mi * B_DH, (mi + 1) * B_DH)
            s = _dot_nt(q[:, sl], k[:, sl])
            if diagonal:
                s = jnp.where(mask, s, -jnp.inf)
            m_old = m_ref[mi]
            m_new = jnp.maximum(m_old, jnp.max(s, axis=-1, keepdims=True))
            alpha = jnp.exp2(m_old - m_new)
            p = jnp.exp2(s - m_new)
            l_ref[mi] = alpha * l_ref[mi] + jnp.sum(p, axis=-1, keepdims=True)
            acc_ref[mi] = alpha * acc_ref[mi] + _dot(p.astype(BF16), v)
            m_ref[mi] = m_new

    @pl.when(ki < qi)
    def _():
        step(False)

    @pl.when(ki == qi)
    def _():
        step(True)
        lam = _lambda(lam_ref, lam_init)
        o = acc_ref[0] / l_ref[0] - lam * (acc_ref[1] / l_ref[1])
        o = o * lax.rsqrt(jnp.mean(o * o, axis=-1, keepdims=True) + EPS) * gsub_ref[...]
        o_ref[...] = (o * (1.0 - lam_init)).astype(o_ref.dtype)


def _attention(q, k, v, lam_b, g_sub, j, lam_init, *, batch, seq, tq=512):
    m = q.shape[0]
    nq = seq // tq
    w = 2 * B_DH
    return pl.pallas_call(
        functools.partial(_attn_body, tq=tq, lam_init=lam_init),
        grid=(batch, B_HEADS, nq, nq),
        in_specs=[
            pl.BlockSpec((tq, w), lambda b, h, qi, ki: (b * nq + qi, h)),
            pl.BlockSpec((tq, w), lambda b, h, qi, ki: (b * nq + jnp.minimum(ki, qi), h)),
            pl.BlockSpec((tq, w), lambda b, h, qi, ki: (b * nq + jnp.minimum(ki, qi), h)),
            pl.BlockSpec((None, 4, B_DH), lambda b, h, qi, ki: (j, 0, 0)),
            pl.BlockSpec((None, 1, B_DV), lambda b, h, qi, ki: (j, 0, 0)),
        ],
        out_specs=pl.BlockSpec((tq, w), lambda b, h, qi, ki: (b * nq + qi, h)),
        out_shape=jax.ShapeDtypeStruct((m, B_HEADS * B_DV), BF16),
        scratch_shapes=[
            pltpu.VMEM((2, tq, 1), F32),
            pltpu.VMEM((2, tq, 1), F32),
            pltpu.VMEM((2, tq, B_DV), F32),
        ],
        compiler_params=_params(("parallel", "parallel", "parallel", "arbitrary")),
        name="diff_attention",
    )(q, k, v, lam_b, g_sub.reshape(-1, 1, B_DV))


QROWS = 16


def _decode_attn_body(pt_ref, q_ref, kn_ref, vn_ref, lam_ref, gsub_ref, *refs, pages, lam_init):
    k_refs = (refs[:pages], refs[pages:2 * pages])
    v_refs = (refs[2 * pages:3 * pages], refs[3 * pages:4 * pages])
    o_ref, qbd_ref, m_ref, l_ref, acc_ref = refs[4 * pages:]
    s = pl.program_id(1)
    rr = lax.broadcasted_iota(jnp.int32, (QROWS, B_DV), 0)
    cc = lax.broadcasted_iota(jnp.int32, (QROWS, B_DV), 1)
    hcols = lambda h: slice(h * B_DV, (h + 1) * B_DV)

    @pl.when(s == 0)
    def _():
        for h in range(B_HEADS):
            qbd_ref[h] = jnp.where(lax.shift_right_logical(cc, 7) == rr, q_ref[:, hcols(h)], 0.0).astype(BF16)
        m_ref[...] = jnp.full(m_ref.shape, -jnp.inf, F32)
        l_ref[...] = jnp.zeros(l_ref.shape, F32)
        acc_ref[...] = jnp.zeros(acc_ref.shape, F32)

    head_rows = lambda h: pl.ds(h, PAGE_SIZE, stride=B_HEADS)
    halves = (slice(0, B_DH), slice(B_DH, 2 * B_DH))
    for p in range(pages):
        sc = [sum(_dot_nt(qbd_ref[h, :, halves[x]], k_refs[x][p][head_rows(h), :].astype(BF16)) for x in (0, 1))
              for h in range(B_HEADS)]
        for h in range(B_HEADS):
            m_old = m_ref[h]
            m_new = jnp.maximum(m_old, jnp.max(sc[h], axis=-1, keepdims=True))
            alpha = jnp.exp(m_old - m_new)
            pr = jnp.exp(sc[h] - m_new)
            l_ref[h] = alpha * l_ref[h] + jnp.sum(pr, axis=-1, keepdims=True)
            pb = pr.astype(BF16)
            for x in (0, 1):
                acc_ref[h, :, halves[x]] = (alpha * acc_ref[h, :, halves[x]]
                                            + _dot(pb, v_refs[x][p][head_rows(h), :].astype(BF16)))
            m_ref[h] = m_new

    @pl.when(s == pl.num_programs(1) - 1)
    def _():
        lam = _lambda(lam_ref, lam_init)
        for h in range(B_HEADS):
            kn = kn_ref[:, hcols(h)].astype(BF16).astype(F32)
            vn = vn_ref[:, hcols(h)].astype(BF16).astype(F32)
            s_new = jnp.sum(qbd_ref[h].astype(F32) * kn, axis=-1, keepdims=True)
            m_old = m_ref[h]
            m_new = jnp.maximum(m_old, s_new)
            alpha = jnp.exp(m_old - m_new)
            pn = jnp.exp(s_new - m_new)
            on = (alpha * acc_ref[h] + pn.astype(BF16).astype(F32) * vn) / (alpha * l_ref[h] + pn)
            o = on[0:1] - lam * on[1:2]
            o = o * lax.rsqrt(jnp.mean(o * o, axis=-1, keepdims=True) + EPS) * gsub_ref[...]
            o_ref[:, hcols(h)] = o * (1.0 - lam_init)


def _decode_attention(q, k_new, v_new, cache_k, cache_v, page_table, lam_b, g_sub, j, lam_init, *, pages=4):
    bsz, width = q.shape
    n_pages = page_table.shape[1]
    nsteps = n_pages // pages
    row3 = lambda a: a.reshape(bsz, 1, width)
    rspec = pl.BlockSpec((None, 1, width), lambda b, s, pt: (b, 0, 0))

    def page_spec(p, half):
        return pl.BlockSpec((None, PAGE_SIZE * B_HEADS, B_DH),
                            lambda b, s, pt: (pt[b * n_pages + s * pages + p], 0, half))

    page_specs = [page_spec(p, half) for half in (0, 1) for p in range(pages)]

    grid_spec = pltpu.PrefetchScalarGridSpec(
        num_scalar_prefetch=1,
        grid=(bsz, nsteps),
        in_specs=[
            rspec, rspec, rspec,
            pl.BlockSpec((None, 4, B_DH), lambda b, s, pt: (j, 0, 0)),
            pl.BlockSpec((None, 1, B_DV), lambda b, s, pt: (j, 0, 0)),
        ] + page_specs * 2,
        out_specs=rspec,
        scratch_shapes=[
            pltpu.VMEM((B_HEADS, QROWS, B_DV), BF16),
            pltpu.VMEM((B_HEADS, QROWS, 1), F32),
            pltpu.VMEM((B_HEADS, QROWS, 1), F32),
            pltpu.VMEM((B_HEADS, QROWS, B_DV), F32),
        ],
    )
    out = pl.pallas_call(
        functools.partial(_decode_attn_body, pages=pages, lam_init=lam_init),
        grid_spec=grid_spec,
        out_shape=jax.ShapeDtypeStruct((bsz, 1, width), F32),
        compiler_params=_params(("parallel", "arbitrary")),
        name="decode_attention",
    )(page_table.reshape(-1), row3(q), row3(k_new), row3(v_new), lam_b, g_sub.reshape(-1, 1, B_DV),
      *([cache_k] * (2 * pages)), *([cache_v] * (2 * pages)))
    return out.reshape(bsz, width)


def _lambda_init(layer):
    return 0.8 - 0.6 * math.exp(-0.3 * layer)


def _rope_tables(pos):
    half = B_DH // 2
    inv = ROPE_THETA ** (-jnp.arange(half, dtype=F32) / half)
    ang = pos.astype(F32)[:, None] * inv
    cos, sin = jnp.cos(ang), jnp.sin(ang)
    return jnp.concatenate([cos, cos], axis=-1), jnp.concatenate([-sin, sin], axis=-1)


def _group_heads(a, heads):
    m = a.shape[0]
    return a.reshape(m, A_HEADS // heads, heads).transpose(1, 0, 2)


def _trunk(x3, mods, mod_kv, weights, *, prompt, s_a, qkv_bufs, ffn_bufs, past):
    (g_mix, g_ffn, w_in_a, conv_a, a_log, dt_bias, g_out_a, w_out_a, g_kv, w_k, w_v, w_q_b, lam_b, g_sub_b,
     w_o_b, w_gate, w_up, conv_f, b_conv_f, w_down, g_final) = weights
    bsz, t, d = x3.shape
    heads = min(8, A_HEADS)
    if prompt:
        nb, nt = bsz, t
        tm, tm_n, tm_dn = min(2048, t), min(512, t), min(1024, t)
        mod3 = lambda mm: mm.reshape(bsz, 1, -1)
        pos = jnp.arange(t, dtype=jnp.int32)
    else:
        nb, nt = 1, bsz
        tm = tm_n = tm_dn = bsz
        mod3 = lambda mm: mm.reshape(1, bsz, -1)
        pos = jnp.full((bsz,), past[3], dtype=jnp.int32)
    m = nb * nt
    x = x3.reshape(m, d)
    rope = _rope_tables(pos)
    new_s, new_qkv, new_ffn = [], [], []
    k_new = v_new = None
    for l in range(DEPTH):
        md = mod3(mods[l])
        h = _norm_mod(x.reshape(nb, nt, d), g_mix, l, md, 0, 1, tm_n, BF16 if prompt else F32).reshape(m, d)
        if l < N_A:
            proj = _matmul(h, w_in_a, layer=l, n_out=QKV_W + A_DVW, tm=tm, tn=512, rows_per_seq=nt, name="w_in")
            w_ab = lax.slice(w_in_a, (l, 0, QKV_W + A_DVW), (l + 1, d, QKV_W + A_DVW + 2 * A_HEADS))[0]
            w_ab = jnp.pad(w_ab, ((0, 0), (0, LANES - 2 * A_HEADS)))
            ab = _matmul(h, w_ab, n_out=LANES, tm=tm, tn=LANES, rows_per_seq=nt, name="w_in_ab")
            if prompt:
                prev = jnp.pad(qkv_bufs[l], ((0, 0), (8 - (CONV_W - 1), 0), (0, 0)))
                new_qkv.append(proj.reshape(bsz, t, -1)[:, t - (CONV_W - 1):, :QKV_W])
            else:
                prev = jnp.swapaxes(qkv_bufs[l], 0, 1)
                new_qkv.append(jnp.concatenate([qkv_bufs[l][:, 1:], proj[:, None, :QKV_W]], axis=1))
            qkv = _qkvconv(proj, conv_a, l, prev, tm=min(512, nt), tn=512, rows_per_seq=nt, seq_rows=prompt)
            g_full, beta_full = _gates(ab, a_log[l], dt_bias[l], tm=min(512, m), cumsum=prompt)
            gc = g_full[:, :A_HEADS]
            beta = beta_full[:, A_HEADS:2 * A_HEADS]
            if prompt:
                qkv_d, proj_d, dseq = qkv, proj, t
            else:
                dseq = 2 * CHUNK
                padrows = lambda a: jnp.pad(a[:, None], ((0, 0), (0, dseq - 1), (0, 0))).reshape(bsz * dseq, -1)
                qkv_d, proj_d, beta = padrows(qkv), padrows(proj), padrows(beta)
                gc = jnp.concatenate([jnp.broadcast_to(gc[:, None], (bsz, CHUNK, A_HEADS)),
                                      jnp.zeros((bsz, CHUNK, A_HEADS), F32)], axis=1).reshape(bsz * dseq, -1)
            og, s_fin = _delta(qkv_d, proj_d, _group_heads(gc, heads), jnp.swapaxes(_group_heads(gc, heads), 1, 2),
                               _group_heads(beta, heads), s_a[l], g_out_a, l, batch=bsz, seq=dseq, heads=heads)
            if not prompt:
                og = og.reshape(bsz, dseq, -1)[:, 0].astype(F32)
            new_s.append(s_fin)
            x = _matmul(og, w_out_a, layer=l, n_out=d, tm=tm, tn=512, rows_per_seq=nt, epi="resid", res=x,
                        mod3=md, k_gate=2, name="w_out")
        else:
            j = l - N_A
            lam_init = _lambda_init(l)
            q = _matmul(h, w_q_b, layer=j, n_out=d, tm=tm, tn=512, rows_per_seq=nt, epi="rope", rope=rope,
                        scale=B_DH ** -0.5 * (math.log2(math.e) if prompt else 1.0),
                        out_dtype=BF16 if prompt else F32, name="w_q")
            if prompt:
                o = _attention(q, k_att, v_att, lam_b, g_sub_b, j, lam_init, batch=bsz, seq=t, tq=min(512, t))
            else:
                o = _decode_attention(q, k_new, v_new, past[0], past[1], past[2], lam_b, g_sub_b, j, lam_init)
            x = _matmul(o, w_o_b, layer=j, n_out=d, tm=tm, tn=512, rows_per_seq=nt, epi="resid", res=x,
                        mod3=md, k_gate=2, name="w_o")
        h = _norm_mod(x.reshape(nb, nt, d), g_ffn, l, md, 3, 4, tm_n, BF16 if prompt else F32).reshape(m, d)
        if prompt:
            prev = jnp.pad(ffn_bufs[l], ((0, 0), (16 - (FFN_CONV_W - 1), 0), (0, 0)))
        else:
            prev = jnp.swapaxes(ffn_bufs[l], 0, 1)
        a, nbuf = _ffn1(h, w_gate, w_up, conv_f, b_conv_f, l, prev, tm=tm, tn=256, rows_per_seq=nt, seq_rows=prompt)
        if prompt:
            new_ffn.append(nbuf[nt // tm - 1::nt // tm, 8 - (FFN_CONV_W - 1):])
        else:
            new_ffn.append(jnp.stack([ffn_bufs[l][:, 1], nbuf[0]], axis=1))
        x = _matmul(a, w_down, layer=l, n_out=d, tm=tm_dn, tn=256, rows_per_seq=nt, epi="resid", res=x, mod3=md,
                    k_gate=5, name="w_down")
        if l == N_A - 1:
            hkv = _norm_mod(x.reshape(nb, nt, d), g_kv, 0, mod3(mod_kv), 0, 1, tm_n, BF16 if prompt else F32)
            hkv = hkv.reshape(m, d)
            kv_dt = (F32, BF16) if prompt else (F32,)
            k_new, *k_att = _matmul(hkv, w_k, n_out=d, tm=tm, tn=512, rows_per_seq=nt, epi="rope", rope=rope,
                                    out_dtype=kv_dt, name="w_k")
            v_new, *v_att = _matmul(hkv, w_v, n_out=d, tm=tm, tn=512, rows_per_seq=nt, out_dtype=kv_dt, name="w_v")
            k_att, v_att = (k_att[0], v_att[0]) if prompt else (None, None)
    y = _final_norm(x.reshape(nb, nt, d), g_final, tm_n).reshape(bsz, t, d)
    return (y, k_new.reshape(bsz, t, B_HEADS, 2 * B_DH), v_new.reshape(bsz, t, B_HEADS, B_DV),
            jnp.stack(new_s), jnp.stack(new_qkv), jnp.stack(new_ffn))


def kernel(x_prompt, x_sample, cache_k, cache_v, state_delta, state_qkv_conv, state_ffn_conv, page_table, c_prompt, c_sample, w_ada, b_ada, g_mix, g_ffn, w_in_a, conv_a, a_log, dt_bias, g_out_a, w_out_a, w_ada_kv, b_ada_kv, g_kv, w_k, w_v, w_q_b, lam_b, g_sub_b, w_o_b, w_gate, w_up, conv_f, b_conv_f, w_down, g_final):
    bp, _, d = x_prompt.shape
    bs = x_sample.shape[0]
    c_all = jnp.concatenate([jnp.pad(c_prompt, ((0, 8 - bp), (0, 0))), c_sample], axis=0)
    mods = _mod_all(c_all, w_ada, b_ada)
    mod_kv = _mod_all(c_all, w_ada_kv[None], b_ada_kv[None])[0]
    weights = (g_mix, g_ffn, w_in_a, conv_a, a_log, dt_bias, g_out_a, w_out_a, g_kv, w_k, w_v, w_q_b, lam_b,
               g_sub_b, w_o_b, w_gate, w_up, conv_f, b_conv_f, w_down, g_final)
    dt = x_prompt.dtype
    yp, kp, vp, sp, qcp, fcp = _trunk(
        x_prompt, mods[:, :bp], mod_kv[:bp], weights, prompt=True,
        s_a=jnp.zeros((N_A, bp, A_HEADS, A_DK, A_DV), state_delta.dtype),
        qkv_bufs=jnp.zeros((N_A, bp, CONV_W - 1, QKV_W), dt),
        ffn_bufs=jnp.zeros((DEPTH, bp, FFN_CONV_W - 1, D_FF), dt), past=None)
    n_pool = cache_k.shape[0]
    past = (cache_k.reshape(n_pool, PAGE_SIZE * B_HEADS, -1), cache_v.reshape(n_pool, PAGE_SIZE * B_HEADS, -1),
            page_table, page_table.shape[1] * PAGE_SIZE)
    ys, ks, vs, ss, qcs, fcs = _trunk(
        x_sample, mods[:, 8:8 + bs], mod_kv[8:8 + bs], weights, prompt=False,
        s_a=state_delta, qkv_bufs=state_qkv_conv, ffn_bufs=state_ffn_conv, past=past)
    return (yp, ys, kp, vp, ks, vs, sp, ss, qcp, qcs, fcp, fcs)
```

```python
import functools
import math

import jax
import jax.numpy as jnp
from jax import lax
from jax.experimental import pallas as pl
from jax.experimental.pallas import tpu as pltpu

F32 = jnp.float32
BF16 = jnp.bfloat16
DT_KK = BF16
DT_STATE = BF16

D_MODEL = 4096
DEPTH = 4
N_A = DEPTH // 2
A_HEADS = 32
A_DK = 128
A_DV = 128
A_DQK = A_HEADS * A_DK
A_DVW = A_HEADS * A_DV
QKV_W = 2 * A_DQK + A_DVW
CONV_W = 4
CHUNK = 64
B_HEADS = 16
B_DH = 128
B_DV = 256
ROPE_THETA = 10000.0
D_FF = 11008
FFN_CONV_W = 3
EPS = 1e-6
PAGE_SIZE = 128

VMEM_LIMIT_BYTES = 60000 * 1024
LANES = 128
FFN_SUB_ROWS = 1024


def _params(sem):
    return pltpu.CompilerParams(dimension_semantics=sem, vmem_limit_bytes=VMEM_LIMIT_BYTES)


def _dot(a, b):
    return jnp.dot(a, b, preferred_element_type=F32)


def _dot_f32(a, b):
    return jnp.dot(a, b, preferred_element_type=F32, precision=lax.Precision.HIGHEST)


def _dot_nt(a, b):
    return lax.dot_general(a, b, (((1,), (1,)), ((), ())), preferred_element_type=F32)


def _dot_tn(a, b):
    return lax.dot_general(a, b, (((0,), (0,)), ((), ())), preferred_element_type=F32)


def _silu(x):
    return x * jax.nn.sigmoid(x)


def _panel_spec(tm, k):
    return pl.BlockSpec((tm, k), lambda i, j: (i, 0), pipeline_mode=pl.Buffered(1))


def _mod_body(c_ref, w_ref, b_ref, o_ref):
    c = c_ref[...]
    o_ref[...] = _dot(_silu(c).astype(BF16), w_ref[...].astype(BF16)) + b_ref[...]


def _mod_all(c_all, w, b, tn=512):
    nl, k, n = w.shape
    mp = c_all.shape[0]
    return pl.pallas_call(
        _mod_body,
        grid=(nl, n // tn),
        in_specs=[
            pl.BlockSpec((mp, k), lambda l, j: (0, 0)),
            pl.BlockSpec((None, k, tn), lambda l, j: (l, 0, j)),
            pl.BlockSpec((None, 1, tn), lambda l, j: (l, 0, j)),
        ],
        out_specs=pl.BlockSpec((None, mp, tn), lambda l, j: (l, 0, j)),
        out_shape=jax.ShapeDtypeStruct((nl, mp, n), F32),
        compiler_params=_params(("parallel", "parallel")),
        name="adaln_mod",
    )(c_all, w, b.reshape(nl, 1, n))


def _norm_mod_body(x_ref, g_ref, sh_ref, sc_ref, o_ref):
    x = x_ref[...]
    y = x * lax.rsqrt(jnp.mean(x * x, axis=-1, keepdims=True) + EPS) * g_ref[...]
    o_ref[...] = (y * (1.0 + sc_ref[...]) + sh_ref[...]).astype(o_ref.dtype)


def _norm_body(x_ref, g_ref, o_ref):
    x = x_ref[...]
    o_ref[...] = (x * lax.rsqrt(jnp.mean(x * x, axis=-1, keepdims=True) + EPS) * g_ref[...]).astype(o_ref.dtype)


def _norm_mod(x3, g, layer, mod3, k_shift, k_scale, tm, out_dtype):
    b, t, d = x3.shape
    g3 = g.reshape(-1, 1, d)
    r = mod3.shape[1]
    rb = 1 if r == 1 else tm
    mspec = lambda k: pl.BlockSpec((None, rb, d), lambda bi, i: (bi, 0 if r == 1 else i, k))
    return pl.pallas_call(
        _norm_mod_body,
        grid=(b, t // tm),
        in_specs=[
            pl.BlockSpec((None, tm, d), lambda bi, i: (bi, i, 0)),
            pl.BlockSpec((None, 1, d), lambda bi, i: (layer, 0, 0)),
            mspec(k_shift),
            mspec(k_scale),
        ],
        out_specs=pl.BlockSpec((None, tm, d), lambda bi, i: (bi, i, 0)),
        out_shape=jax.ShapeDtypeStruct((b, t, d), out_dtype),
        compiler_params=_params(("parallel", "parallel")),
        name="norm_mod",
    )(x3, g3, mod3, mod3)


def _final_norm(x3, g, tm):
    b, t, d = x3.shape
    return pl.pallas_call(
        _norm_body,
        grid=(b, t // tm),
        in_specs=[
            pl.BlockSpec((None, tm, d), lambda bi, i: (bi, i, 0)),
            pl.BlockSpec((1, d), lambda bi, i: (0, 0)),
        ],
        out_specs=pl.BlockSpec((None, tm, d), lambda bi, i: (bi, i, 0)),
        out_shape=jax.ShapeDtypeStruct((b, t, d), F32),
        compiler_params=_params(("parallel", "parallel")),
        name="final_norm",
    )(x3, g.reshape(1, d))


def _epilogue(acc, epi, refs, scale):
    if epi == "plain":
        return acc
    if epi == "resid":
        res_ref, gt_ref = refs
        return res_ref[...] + gt_ref[...] * acc
    cs = refs[0][...]
    sn = refs[1][...]
    parts = []
    for gi in range(acc.shape[1] // LANES):
        a = acc[:, gi * LANES:(gi + 1) * LANES]
        parts.append(a * cs + pltpu.roll(a, LANES // 2, axis=1) * sn)
    out = jnp.concatenate(parts, axis=1)
    return out * scale if scale != 1.0 else out


def _mm_body(x_ref, w_ref, xs_ref, *rest, epi, scale, scale_s, n_outs, w_t):
    n_epi = 0 if epi == "plain" else 2
    epi_refs, epi_refs_s = rest[:n_epi], rest[n_epi:2 * n_epi]
    o_refs = rest[2 * n_epi:2 * n_epi + n_outs]
    os_ref = rest[2 * n_epi + n_outs]
    dot = _dot_nt if w_t else _dot
    wb = w_ref[...].astype(BF16)
    out = _epilogue(dot(x_ref[...].astype(BF16), wb), epi, epi_refs, scale)
    for o_ref in o_refs:
        o_ref[...] = out.astype(o_ref.dtype)

    @pl.when(pl.program_id(0) == 0)
    def _():
        os_ref[...] = _epilogue(dot(xs_ref[...].astype(BF16), wb), epi, epi_refs_s, scale_s)

    @pl.when(pl.program_id(0) != 0)
    def _():
        os_ref[...] = jnp.zeros(os_ref.shape, F32)


def _matmul(x, xs, w, *, layer=0, col_off=0, n_out, tm, tn, rows_per_seq, w_t=False, epi="plain", res=None,
            res_s=None, mod3=None, mod3_s=None, k_gate=0, rope=None, rope_s=None, scale=1.0, scale_s=1.0,
            out_dtype=F32, name="matmul"):
    m, k = x.shape
    ms = xs.shape[0]
    tps = rows_per_seq // tm
    off = col_off // tn
    if w_t:
        wspec = pl.BlockSpec((None, tn, k), lambda i, j: (layer, j + off, 0))
    elif w.ndim == 3:
        wspec = pl.BlockSpec((None, k, tn), lambda i, j: (layer, 0, j + off))
    else:
        wspec = pl.BlockSpec((k, tn), lambda i, j: (0, j + off))
    in_specs = [_panel_spec(tm, k), wspec, pl.BlockSpec((ms, k), lambda i, j: (0, 0))]
    args = [x, w, xs]
    if epi == "resid":
        goff = k_gate * n_out // tn
        in_specs += [
            pl.BlockSpec((tm, tn), lambda i, j: (i, j)),
            pl.BlockSpec((None, 1, tn), lambda i, j: (i // tps, 0, goff + j)),
            pl.BlockSpec((ms, tn), lambda i, j: (0, j)),
            pl.BlockSpec((None, ms, tn), lambda i, j: (0, 0, goff + j)),
        ]
        args += [res, mod3, res_s, mod3_s]
    elif epi == "rope":
        in_specs += [pl.BlockSpec((tm, LANES), lambda i, j: (i % tps, 0))] * 2
        in_specs += [pl.BlockSpec((ms, LANES), lambda i, j: (0, 0))] * 2
        args += list(rope) + list(rope_s)
    dtypes = out_dtype if isinstance(out_dtype, tuple) else (out_dtype,)
    *outs, out_s = pl.pallas_call(
        functools.partial(_mm_body, epi=epi, scale=scale, scale_s=scale_s, n_outs=len(dtypes), w_t=w_t),
        grid=(m // tm, n_out // tn),
        in_specs=in_specs,
        out_specs=[pl.BlockSpec((tm, tn), lambda i, j: (i, j))] * len(dtypes)
        + [pl.BlockSpec((None, ms, tn), lambda i, j: (i, 0, j))],
        out_shape=[jax.ShapeDtypeStruct((m, n_out), dt) for dt in dtypes]
        + [jax.ShapeDtypeStruct((m // tm, ms, n_out), F32)],
        compiler_params=_params(("arbitrary", "arbitrary")),
        name=name,
    )(*args)
    return (tuple(outs) if isinstance(out_dtype, tuple) else outs[0]), out_s[0]


def _ffn1_body(x_ref, wg_ref, wu_ref, cw_ref, cb_ref, p_ref, xs_ref, ps_ref, a_ref, nb_ref, as_ref, gs_ref,
               carry_ref, *, tps):
    i = pl.program_id(0)
    j = pl.program_id(1)
    wg = wg_ref[...].astype(BF16)
    wu = wu_ref[...].astype(BF16)
    cw = cw_ref[...]
    cb = cb_ref[...]
    tm, tn = a_ref.shape

    @pl.when(i == 0)
    def _():
        carry_ref[j] = jnp.zeros((16, tn), F32)
        xs = xs_ref[...].astype(BF16)
        g = _dot(xs, wg)
        gs_ref[...] = g
        y = cw[2:3] * g + cw[1:2] * ps_ref[1] + cw[0:1] * ps_ref[0] + cb
        as_ref[...] = _silu(y) * _dot(xs, wu)

    @pl.when(i != 0)
    def _():
        gs_ref[...] = jnp.zeros(gs_ref.shape, F32)
        as_ref[...] = jnp.zeros(as_ref.shape, F32)

    prev = jnp.where((i % tps) == 0, p_ref[...], carry_ref[j])
    r16 = lax.broadcasted_iota(jnp.int32, (16, tn), 0)
    sub = min(tm, FFN_SUB_ROWS)
    for s in range(tm // sub):
        rs = slice(s * sub, (s + 1) * sub)
        x = x_ref[rs, :].astype(BF16)
        g = _dot(x, wg)
        u = _dot(x, wu)
        head = g[0:16]

        def shifted(k):
            h16 = jnp.where(r16 < k, pltpu.roll(prev, k, axis=0), pltpu.roll(head, k, axis=0))
            return jnp.concatenate([h16, pltpu.roll(g, k, axis=0)[16:]], axis=0)

        y = cw[2:3] * g + cw[1:2] * shifted(1) + cw[0:1] * shifted(2) + cb
        a_ref[rs, :] = (_silu(y) * u).astype(a_ref.dtype)
        prev = g[sub - 16:sub]
    carry_ref[j] = prev
    nb_ref[...] = prev[8:16]


def _ffn1(h, hs, w_gate, w_up, conv_f, b_conv_f, layer, prev, prev_s, *, tm, tn, rows_per_seq):
    m, k = h.shape
    ms = hs.shape[0]
    n = D_FF
    tps = rows_per_seq // tm
    sspec = pl.BlockSpec((None, ms, tn), lambda i, j: (i, 0, j))
    a, nb, a_s, g_s = pl.pallas_call(
        functools.partial(_ffn1_body, tps=tps),
        grid=(m // tm, n // tn),
        in_specs=[
            _panel_spec(tm, k),
            pl.BlockSpec((None, k, tn), lambda i, j: (layer, 0, j)),
            pl.BlockSpec((None, k, tn), lambda i, j: (layer, 0, j)),
            pl.BlockSpec((None, FFN_CONV_W, tn), lambda i, j: (layer, 0, j)),
            pl.BlockSpec((None, 1, tn), lambda i, j: (layer, 0, j)),
            pl.BlockSpec((None, 16, tn), lambda i, j: (i // tps, 0, j)),
            pl.BlockSpec((ms, k), lambda i, j: (0, 0)),
            pl.BlockSpec((2, ms, tn), lambda i, j: (0, 0, j)),
        ],
        out_specs=[
            pl.BlockSpec((tm, tn), lambda i, j: (i, j)),
            pl.BlockSpec((None, 8, tn), lambda i, j: (i, 0, j)),
            sspec,
            sspec,
        ],
        out_shape=[
            jax.ShapeDtypeStruct((m, n), BF16),
            jax.ShapeDtypeStruct((m // tm, 8, n), F32),
            jax.ShapeDtypeStruct((m // tm, ms, n), F32),
            jax.ShapeDtypeStruct((m // tm, ms, n), F32),
        ],
        scratch_shapes=[pltpu.VMEM((n // tn, 16, tn), F32)],
        compiler_params=_params(("arbitrary", "arbitrary")),
        name="ffn_gate_up",
    )(h, w_gate, w_up, conv_f, b_conv_f.reshape(DEPTH, 1, n), prev, hs, prev_s)
    return a, nb, a_s[0], g_s[0]


def _qkvconv_body(x_ref, cw_ref, p_ref, o_ref, *scratch, tps, seq_rows, q_blocks, qk_blocks):
    jn = pl.program_id(0)
    i = pl.program_id(1)
    x = x_ref[...]
    cw = cw_ref[...]
    tm, tn = x.shape
    if seq_rows:
        (carry_ref,) = scratch

        @pl.when(i == 0)
        def _():
            carry_ref[...] = jnp.zeros((8, tn), F32)

        prev = jnp.where((i % tps) == 0, p_ref[...], carry_ref[...])
        r8 = lax.broadcasted_iota(jnp.int32, (8, tn), 0)
        head = x[0:8]

        def shifted(s):
            h8 = jnp.where(r8 < s, pltpu.roll(prev, s, axis=0), pltpu.roll(head, s, axis=0))
            return jnp.concatenate([h8, pltpu.roll(x, s, axis=0)[8:]], axis=0)

        x1, x2, x3 = shifted(1), shifted(2), shifted(3)
        carry_ref[...] = x[tm - 8:tm]
    else:
        x1, x2, x3 = p_ref[2], p_ref[1], p_ref[0]
    y = _silu(cw[3:4] * x + cw[2:3] * x1 + cw[1:2] * x2 + cw[0:1] * x3)
    mult = jnp.where(jn < q_blocks, A_DK ** -0.5, 1.0)
    is_qk = jn < qk_blocks
    parts = []
    for gi in range(tn // LANES):
        z = y[:, gi * LANES:(gi + 1) * LANES]
        nrm = lax.rsqrt(jnp.sum(z * z, axis=-1, keepdims=True) + EPS) * mult
        parts.append(z * jnp.where(is_qk, nrm, 1.0))
    o_ref[...] = jnp.concatenate(parts, axis=1)


def _qkvconv(proj, conv_a, layer, prev, *, tm, tn, rows_per_seq, seq_rows):
    m = proj.shape[0]
    tps = rows_per_seq // tm
    if seq_rows:
        pspec = pl.BlockSpec((None, 8, tn), lambda jn, i: (i // tps, 0, jn))
        scratch = [pltpu.VMEM((8, tn), F32)]
    else:
        pspec = pl.BlockSpec((CONV_W - 1, m, tn), lambda jn, i: (0, 0, jn))
        scratch = []
    return pl.pallas_call(
        functools.partial(_qkvconv_body, tps=tps, seq_rows=seq_rows, q_blocks=A_DQK // tn,
                          qk_blocks=2 * A_DQK // tn),
        grid=(QKV_W // tn, m // tm),
        in_specs=[
            pl.BlockSpec((tm, tn), lambda jn, i: (i, jn)),
            pl.BlockSpec((None, CONV_W, tn), lambda jn, i: (layer, 0, jn)),
            pspec,
        ],
        out_specs=pl.BlockSpec((tm, tn), lambda jn, i: (i, jn)),
        out_shape=jax.ShapeDtypeStruct((m, QKV_W), F32),
        scratch_shapes=scratch,
        compiler_params=_params(("arbitrary", "arbitrary")),
        name="qkv_conv",
    )(proj, conv_a, prev)


def _gates_body(ab_ref, alog_ref, dtb_ref, g_ref, beta_ref, *, cumsum):
    ab = ab_ref[...]
    x = ab + dtb_ref[...]
    softplus = jnp.maximum(x, 0.0) + jnp.log1p(jnp.exp(-jnp.abs(x)))
    g = -jnp.exp(alog_ref[...]) * softplus
    if cumsum:
        r = lax.broadcasted_iota(jnp.int32, g.shape, 0) & (CHUNK - 1)
        s = 1
        while s < CHUNK:
            g = g + jnp.where(r >= s, pltpu.roll(g, s, axis=0), 0.0)
            s *= 2
    g_ref[...] = g
    beta_ref[...] = jax.nn.sigmoid(ab)


def _gates(ab, a_log, dt_bias, *, tm, cumsum):
    m, width = ab.shape
    pad = lambda v: jnp.pad(v.astype(F32), (0, width - A_HEADS)).reshape(1, width)
    return pl.pallas_call(
        functools.partial(_gates_body, cumsum=cumsum),
        grid=(m // tm,),
        in_specs=[
            pl.BlockSpec((tm, width), lambda i: (i, 0)),
            pl.BlockSpec((1, width), lambda i: (0, 0)),
            pl.BlockSpec((1, width), lambda i: (0, 0)),
        ],
        out_specs=[pl.BlockSpec((tm, width), lambda i: (i, 0))] * 2,
        out_shape=[jax.ShapeDtypeStruct((m, width), F32)] * 2,
        compiler_params=_params(("parallel",)),
        name="delta_gates",
    )(ab, pad(a_log), pad(dt_bias))


def _delta_body(q_ref, k_ref, v_ref, gate_ref, gc_ref, gct_ref, beta_ref, s0_ref, gout_ref, og_ref, s_ref, *,
                heads, chunks):
    @pl.when(pl.program_id(2) == 0)
    def _():
        s_ref[...] = s0_ref[...]

    c = CHUNK
    ii = lax.broadcasted_iota(jnp.int32, (c, c), 0)
    jj = lax.broadcasted_iota(jnp.int32, (c, c), 1)
    eye = jnp.where(ii == jj, 1.0, 0.0)
    gout = gout_ref[...]
    units = [(ci, hh) for ci in range(chunks) for hh in range(heads)]
    rows = lambda ci: slice(ci * c, (ci + 1) * c)
    cols = lambda hh: slice(hh * A_DK, (hh + 1) * A_DK)
    q = [q_ref[rows(ci), cols(hh)] for ci, hh in units]
    k = [k_ref[rows(ci), cols(hh)] for ci, hh in units]
    gcol = [gc_ref[rows(ci), hh:hh + 1] for ci, hh in units]
    bcol = [beta_ref[rows(ci), hh:hh + 1] for ci, hh in units]
    kb = [x.astype(DT_KK) for x in k]
    kk = [_dot_nt(x, x) for x in kb]
    qk = [_dot_nt(x.astype(DT_KK), y) for x, y in zip(q, kb)]
    a_kk, a_qk, rhs, eg = [], [], [], []
    for u, (ci, hh) in enumerate(units):
        dec = jnp.exp(jnp.where(ii >= jj, gcol[u] - gct_ref[hh:hh + 1, rows(ci)], -jnp.inf))
        a_kk.append(jnp.where(ii > jj, kk[u] * dec, 0.0) * bcol[u])
        a_qk.append((qk[u] * dec).astype(DT_STATE))
        eg.append(jnp.exp(gcol[u]))
        rhs.append(jnp.concatenate([v_ref[rows(ci), cols(hh)] * bcol[u], k[u] * (bcol[u] * eg[u])], axis=1))
    pw = [(-a).astype(BF16) for a in a_kk]
    t_inv = [eye - a for a in a_kk]
    for _ in range(5):
        pw = [_dot(p, p).astype(BF16) for p in pw]
        t_inv = [t + _dot(t.astype(BF16), p) for t, p in zip(t_inv, pw)]
    tb = [t.astype(BF16) for t in t_inv]
    y0 = [_dot(t, r.astype(BF16)) for t, r in zip(tb, rhs)]
    y_hi = [y.astype(BF16) for y in y0]
    a_hi = [a.astype(BF16) for a in a_kk]
    ay = [_dot(a, jnp.concatenate([yh, (y - yh.astype(F32)).astype(BF16)], axis=1))
          for a, y, yh in zip(a_hi, y0, y_hi)]
    al = [_dot((a - ah.astype(F32)).astype(BF16), yh) for a, ah, yh in zip(a_kk, a_hi, y_hi)]
    resid = [r - y - (p[:, :2 * A_DV] + p[:, 2 * A_DV:] + l) for r, y, p, l in zip(rhs, y0, ay, al)]
    ysol = [y + _dot(t, r.astype(BF16)) for y, t, r in zip(y0, tb, resid)]
    for ci in range(chunks):
        us = [u for u, (cu, _) in enumerate(units) if cu == ci]
        s_old = [s_ref[units[u][1]] for u in us]
        sb = [x.astype(DT_STATE) for x in s_old]
        w_s = [_dot(ysol[u][:, A_DV:].astype(DT_STATE), x) for u, x in zip(us, sb)]
        q_s = [_dot((q[u] * eg[u]).astype(DT_STATE), x) for u, x in zip(us, sb)]
        vnb = [(ysol[u][:, :A_DV] - w).astype(DT_STATE) for u, w in zip(us, w_s)]
        glast = [gc_ref[ci * c + c - 1:ci * c + c, units[u][1]:units[u][1] + 1] for u in us]
        kdec = [(k[u] * jnp.exp(g - gcol[u])).astype(DT_STATE) for u, g in zip(us, glast)]
        o_s = [x + _dot(a_qk[u], vn) for u, x, vn in zip(us, q_s, vnb)]
        s_up = [_dot_tn(kd, vn) for kd, vn in zip(kdec, vnb)]
        for n, u in enumerate(us):
            hh = units[u][1]
            s_ref[hh] = jnp.exp(glast[n]) * s_old[n] + s_up[n]
            o = o_s[n]
            on = o * lax.rsqrt(jnp.mean(o * o, axis=-1, keepdims=True) + EPS) * gout
            og_ref[rows(ci), cols(hh)] = (on * _silu(gate_ref[rows(ci), cols(hh)])).astype(og_ref.dtype)


def _delta(qkv, proj, gc_g, gct_g, beta_g, s0, g_out, layer, *, batch, seq, heads=4, chunks=2):
    m = qkv.shape[0]
    rows = chunks * CHUNK
    nsteps = seq // rows
    w = heads * A_DK
    qo, ko, vo, go = 0, A_DQK // w, 2 * A_DQK // w, QKV_W // w
    blk = lambda off: pl.BlockSpec((rows, w), lambda b, hg, t: (b * nsteps + t, off + hg))
    return pl.pallas_call(
        functools.partial(_delta_body, heads=heads, chunks=chunks),
        grid=(batch, A_HEADS // heads, nsteps),
        in_specs=[
            blk(qo), blk(ko), blk(vo), blk(go),
            pl.BlockSpec((None, rows, heads), lambda b, hg, t: (hg, b * nsteps + t, 0)),
            pl.BlockSpec((None, heads, rows), lambda b, hg, t: (hg, 0, b * nsteps + t)),
            pl.BlockSpec((None, rows, heads), lambda b, hg, t: (hg, b * nsteps + t, 0)),
            pl.BlockSpec((None, heads, A_DK, A_DV), lambda b, hg, t: (b, hg, 0, 0)),
            pl.BlockSpec((None, 1, A_DV), lambda b, hg, t: (layer, 0, 0)),
        ],
        out_specs=[
            pl.BlockSpec((rows, w), lambda b, hg, t: (b * nsteps + t, hg)),
            pl.BlockSpec((None, heads, A_DK, A_DV), lambda b, hg, t: (b, hg, 0, 0)),
        ],
        out_shape=[
            jax.ShapeDtypeStruct((m, A_DVW), BF16),
            jax.ShapeDtypeStruct((batch, A_HEADS, A_DK, A_DV), F32),
        ],
        compiler_params=_params(("parallel", "parallel", "arbitrary")),
        name="delta_rule",
    )(qkv, qkv, qkv, proj, gc_g, gct_g, beta_g, s0, g_out.reshape(N_A, 1, A_DV))


def _lambda(lam_ref, lam_init):
    lv = lam_ref[...]
    return (jnp.exp(jnp.sum(lv[0:1] * lv[1:2], axis=-1, keepdims=True))
            - jnp.exp(jnp.sum(lv[2:3] * lv[3:4], axis=-1, keepdims=True)) + lam_init)


def _attn_body(q_ref, k_ref, v_ref, lam_ref, gsub_ref, o_ref, m_ref, l_ref, acc_ref, *, tq, lam_init):
    qi = pl.program_id(2)
    ki = pl.program_id(3)

    @pl.when(ki == 0)
    def _():
        m_ref[...] = jnp.full(m_ref.shape, -jnp.inf, F32)
        l_ref[...] = jnp.zeros(l_ref.shape, F32)
        acc_ref[...] = jnp.zeros(acc_ref.shape, F32)

    def step(diagonal):
        q = q_ref[...].astype(BF16)
        k = k_ref[...].astype(BF16)
        v = v_ref[...].astype(BF16)
        if diagonal:
            mask = (lax.broadcasted_iota(jnp.int32, (tq, tq), 1) <= lax.broadcasted_iota(jnp.int32, (tq, tq), 0))
        for mi in range(2):
            sl = slice(mi * B_DH, (mi + 1) * B_DH)
            s = _dot_nt(q[:, sl], k[:, sl])
            if diagonal:
                s = jnp.where(mask, s, -jnp.inf)
            m_old = m_ref[mi]
            m_new = jnp.maximum(m_old, jnp.max(s, axis=-1, keepdims=True))
            alpha = jnp.exp2(m_old - m_new)
            p = jnp.exp2(s - m_new)
            l_ref[mi] = alpha * l_ref[mi] + jnp.sum(p, axis=-1, keepdims=True)
            acc_ref[mi] = alpha * acc_ref[mi] + _dot(p.astype(BF16), v)
            m_ref[mi] = m_new

    @pl.when(ki < qi)
    def _():
        step(False)

    @pl.when(ki == qi)
    def _():
        step(True)
        lam = _lambda(lam_ref, lam_init)
        o = acc_ref[0] / l_ref[0] - lam * (acc_ref[1] / l_ref[1])
        o = o * lax.rsqrt(jnp.mean(o * o, axis=-1, keepdims=True) + EPS) * gsub_ref[...]
        o_ref[...] = (o * (1.0 - lam_init)).astype(o_ref.dtype)


def _attention(q, k, v, lam_b, g_sub, j, lam_init, *, batch, seq, tq=512):
    m = q.shape[0]
    nq = seq // tq
    w = 2 * B_DH
    return pl.pallas_call(
        functools.partial(_attn_body, tq=tq, lam_init=lam_init),
        grid=(batch, B_HEADS, nq, nq),
        in_specs=[
            pl.BlockSpec((tq, w), lambda b, h, qi, ki: (b * nq + qi, h)),
            pl.BlockSpec((tq, w), lambda b, h, qi, ki: (b * nq + jnp.minimum(ki, qi), h)),
            pl.BlockSpec((tq, w), lambda b, h, qi, ki: (b * nq + jnp.minimum(ki, qi), h)),
            pl.BlockSpec((None, 4, B_DH), lambda b, h, qi, ki: (j, 0, 0)),
            pl.BlockSpec((None, 1, B_DV), lambda b, h, qi, ki: (j, 0, 0)),
        ],
        out_specs=pl.BlockSpec((tq, w), lambda b, h, qi, ki: (b * nq + qi, h)),
        out_shape=jax.ShapeDtypeStruct((m, B_HEADS * B_DV), BF16),
        scratch_shapes=[
            pltpu.VMEM((2, tq, 1), F32),
            pltpu.VMEM((2, tq, 1), F32),
            pltpu.VMEM((2, tq, B_DV), F32),
        ],
        compiler_params=_params(("parallel", "parallel", "parallel", "arbitrary")),
        name="diff_attention",
    )(q, k, v, lam_b, g_sub.reshape(-1, 1, B_DV))


QROWS = 16


def _decode_attn_body(pt_ref, q_ref, kn_ref, vn_ref, lam_ref, gsub_ref, *refs, pages, lam_init):
    k_refs = (refs[:pages], refs[pages:2 * pages])
    v_refs = (refs[2 * pages:3 * pages], refs[3 * pages:4 * pages])
    o_ref, qbd_ref, m_ref, l_ref, acc_ref = refs[4 * pages:]
    s = pl.program_id(1)
    rr = lax.broadcasted_iota(jnp.int32, (QROWS, B_DV), 0)
    cc = lax.broadcasted_iota(jnp.int32, (QROWS, B_DV), 1)
    hcols = lambda h: slice(h * B_DV, (h + 1) * B_DV)

    @pl.when(s == 0)
    def _():
        for h in range(B_HEADS):
            qbd_ref[h] = jnp.where(lax.shift_right_logical(cc, 7) == rr, q_ref[:, hcols(h)], 0.0).astype(BF16)
        m_ref[...] = jnp.full(m_ref.shape, -jnp.inf, F32)
        l_ref[...] = jnp.zeros(l_ref.shape, F32)
        acc_ref[...] = jnp.zeros(acc_ref.shape, F32)

    head_rows = lambda h: pl.ds(h, PAGE_SIZE, stride=B_HEADS)
    halves = (slice(0, B_DH), slice(B_DH, 2 * B_DH))
    for p in range(pages):
        sc = [sum(_dot_nt(qbd_ref[h, :, halves[x]], k_refs[x][p][head_rows(h), :].astype(BF16)) for x in (0, 1))
              for h in range(B_HEADS)]
        for h in range(B_HEADS):
            m_old = m_ref[h]
            m_new = jnp.maximum(m_old, jnp.max(sc[h], axis=-1, keepdims=True))
            alpha = jnp.exp(m_old - m_new)
            pr = jnp.exp(sc[h] - m_new)
            l_ref[h] = alpha * l_ref[h] + jnp.sum(pr, axis=-1, keepdims=True)
            pb = pr.astype(BF16)
            for x in (0, 1):
                acc_ref[h, :, halves[x]] = (alpha * acc_ref[h, :, halves[x]]
                                            + _dot(pb, v_refs[x][p][head_rows(h), :].astype(BF16)))
            m_ref[h] = m_new

    @pl.when(s == pl.num_programs(1) - 1)
    def _():
        lam = _lambda(lam_ref, lam_init)
        for h in range(B_HEADS):
            kn = kn_ref[:, hcols(h)].astype(BF16).astype(F32)
            vn = vn_ref[:, hcols(h)].astype(BF16).astype(F32)
            s_new = jnp.sum(qbd_ref[h].astype(F32) * kn, axis=-1, keepdims=True)
            m_old = m_ref[h]
            m_new = jnp.maximum(m_old, s_new)
            alpha = jnp.exp(m_old - m_new)
            pn = jnp.exp(s_new - m_new)
            on = (alpha * acc_ref[h] + pn.astype(BF16).astype(F32) * vn) / (alpha * l_ref[h] + pn)
            o = on[0:1] - lam * on[1:2]
            o = o * lax.rsqrt(jnp.mean(o * o, axis=-1, keepdims=True) + EPS) * gsub_ref[...]
            o_ref[:, hcols(h)] = o * (1.0 - lam_init)


def _decode_attention(q, k_new, v_new, cache_k, cache_v, page_table, lam_b, g_sub, j, lam_init, *, pages=4):
    bsz, width = q.shape
    n_pages = page_table.shape[1]
    nsteps = n_pages // pages
    row3 = lambda a: a.reshape(bsz, 1, width)
    rspec = pl.BlockSpec((None, 1, width), lambda b, s, pt: (b, 0, 0))

    def page_spec(p, half):
        return pl.BlockSpec((None, PAGE_SIZE * B_HEADS, B_DH),
                            lambda b, s, pt: (pt[b * n_pages + s * pages + p], 0, half))

    page_specs = [page_spec(p, half) for half in (0, 1) for p in range(pages)]

    grid_spec = pltpu.PrefetchScalarGridSpec(
        num_scalar_prefetch=1,
        grid=(bsz, nsteps),
        in_specs=[
            rspec, rspec, rspec,
            pl.BlockSpec((None, 4, B_DH), lambda b, s, pt: (j, 0, 0)),
            pl.BlockSpec((None, 1, B_DV), lambda b, s, pt: (j, 0, 0)),
        ] + page_specs * 2,
        out_specs=rspec,
        scratch_shapes=[
            pltpu.VMEM((B_HEADS, QROWS, B_DV), BF16),
            pltpu.VMEM((B_HEADS, QROWS, 1), F32),
            pltpu.VMEM((B_HEADS, QROWS, 1), F32),
            pltpu.VMEM((B_HEADS, QROWS, B_DV), F32),
        ],
    )
    out = pl.pallas_call(
        functools.partial(_decode_attn_body, pages=pages, lam_init=lam_init),
        grid_spec=grid_spec,
        out_shape=jax.ShapeDtypeStruct((bsz, 1, width), F32),
        compiler_params=_params(("parallel", "arbitrary")),
        name="decode_attention",
    )(page_table.reshape(-1), row3(q), row3(k_new), row3(v_new), lam_b, g_sub.reshape(-1, 1, B_DV),
      *([cache_k] * (2 * pages)), *([cache_v] * (2 * pages)))
    return out.reshape(bsz, width)


def _lambda_init(layer):
    return 0.8 - 0.6 * math.exp(-0.3 * layer)


def _rope_tables(pos):
    half = B_DH // 2
    inv = ROPE_THETA ** (-jnp.arange(half, dtype=F32) / half)
    ang = pos.astype(F32)[:, None] * inv
    cos, sin = jnp.cos(ang), jnp.sin(ang)
    return jnp.concatenate([cos, cos], axis=-1), jnp.concatenate([-sin, sin], axis=-1)


def _group_heads(a, heads):
    m = a.shape[0]
    return a.reshape(m, A_HEADS // heads, heads).transpose(1, 0, 2)


def _delta_mixer_mid(proj, ab, l, conv_a, a_log, dt_bias, g_out_a, qkv_buf, s0, *, prompt, bsz, t):
    heads = min(8, A_HEADS)
    m = proj.shape[0]
    if prompt:
        prev = jnp.pad(qkv_buf, ((0, 0), (8 - (CONV_W - 1), 0), (0, 0)))
        new_buf = proj.reshape(bsz, t, -1)[:, t - (CONV_W - 1):, :QKV_W]
    else:
        prev = jnp.swapaxes(qkv_buf, 0, 1)
        new_buf = jnp.concatenate([qkv_buf[:, 1:], proj[:, None, :QKV_W]], axis=1)
    rows_per_seq = t if prompt else m
    qkv = _qkvconv(proj, conv_a, l, prev, tm=min(512, rows_per_seq), tn=512, rows_per_seq=rows_per_seq,
                   seq_rows=prompt)
    g_full, beta_full = _gates(ab, a_log[l], dt_bias[l], tm=min(512, m), cumsum=prompt)
    gc = g_full[:, :A_HEADS]
    beta = beta_full[:, A_HEADS:2 * A_HEADS]
    if prompt:
        qkv_d, proj_d, dseq = qkv, proj, t
    else:
        dseq = 2 * CHUNK
        padrows = lambda a: jnp.pad(a[:, None], ((0, 0), (0, dseq - 1), (0, 0))).reshape(bsz * dseq, -1)
        qkv_d, proj_d, beta = padrows(qkv), padrows(proj), padrows(beta)
        gc = jnp.concatenate([jnp.broadcast_to(gc[:, None], (bsz, CHUNK, A_HEADS)),
                              jnp.zeros((bsz, CHUNK, A_HEADS), F32)], axis=1).reshape(bsz * dseq, -1)
    og, s_fin = _delta(qkv_d, proj_d, _group_heads(gc, heads), jnp.swapaxes(_group_heads(gc, heads), 1, 2),
                       _group_heads(beta, heads), s0, g_out_a, l, batch=bsz, seq=dseq, heads=heads)
    if not prompt:
        og = og.reshape(bsz, dseq, -1)[:, 0].astype(F32)
    return og, new_buf, s_fin


def kernel(x_prompt, x_sample, cache_k, cache_v, state_delta, state_qkv_conv, state_ffn_conv, page_table, c_prompt, c_sample, w_ada, b_ada, g_mix, g_ffn, w_in_a, conv_a, a_log, dt_bias, g_out_a, w_out_a, w_ada_kv, b_ada_kv, g_kv, w_k, w_v, w_q_b, lam_b, g_sub_b, w_o_b, w_gate, w_up, conv_f, b_conv_f, w_down, g_final):
    bp, t, d = x_prompt.shape
    bs = x_sample.shape[0]
    mp = bp * t
    tm, tm_n, tm_dn = min(2048, t), min(512, t), min(1024, t)
    c_all = jnp.concatenate([jnp.pad(c_prompt, ((0, 8 - bp), (0, 0))), c_sample], axis=0)
    mods = _mod_all(c_all, w_ada, b_ada)
    mod_kv = _mod_all(c_all, w_ada_kv[None], b_ada_kv[None])[0]
    mod3_p = lambda mm: mm[:bp].reshape(bp, 1, -1)
    mod3_s = lambda mm: mm[8:8 + bs].reshape(1, bs, -1)
    n_pool = cache_k.shape[0]
    past_len = page_table.shape[1] * PAGE_SIZE
    pages_k = cache_k.reshape(n_pool, PAGE_SIZE * B_HEADS, -1)
    pages_v = cache_v.reshape(n_pool, PAGE_SIZE * B_HEADS, -1)
    rope_p = _rope_tables(jnp.arange(t, dtype=jnp.int32))
    rope_s = _rope_tables(jnp.full((bs,), past_len, dtype=jnp.int32))
    zero_qkv = jnp.zeros((bp, CONV_W - 1, QKV_W), F32)
    zero_ffn = jnp.zeros((bp, FFN_CONV_W - 1, D_FF), F32)
    zero_s = jnp.zeros((bp, A_HEADS, A_DK, A_DV), state_delta.dtype)
    w_in_t = jnp.swapaxes(w_in_a, 1, 2)

    def norm(xp, xs, gain, layer, md_p, md_s, k_shift, k_scale):
        hp = _norm_mod(xp.reshape(bp, t, d), gain, layer, md_p, k_shift, k_scale, tm_n, BF16).reshape(mp, d)
        hs = _norm_mod(xs.reshape(1, bs, d), gain, layer, md_s, k_shift, k_scale, bs, F32).reshape(bs, d)
        return hp, hs

    def mm(xp, xs, w, **kw):
        return _matmul(xp, xs, w, tm=kw.pop("tm", tm), tn=kw.pop("tn", 512), rows_per_seq=t, **kw)

    xp = x_prompt.reshape(mp, d)
    xs = x_sample.reshape(bs, d)
    sp, ss, qcp, qcs, fcp, fcs = [], [], [], [], [], []
    for l in range(DEPTH):
        md_p, md_s = mod3_p(mods[l]), mod3_s(mods[l])
        resid = dict(epi="resid", mod3=md_p, mod3_s=md_s)
        hp, hs = norm(xp, xs, g_mix, l, md_p, md_s, 0, 1)
        if l < N_A:
            proj_p, proj_s = mm(hp, hs, w_in_t, layer=l, w_t=True, n_out=QKV_W + A_DVW, name="w_in")
            ab_p, ab_s = mm(hp, hs, w_in_t, layer=l, w_t=True, col_off=QKV_W + A_DVW, n_out=2 * A_HEADS,
                            tn=2 * A_HEADS, name="w_in_ab")
            og_p, buf, s_fin = _delta_mixer_mid(proj_p, ab_p, l, conv_a, a_log, dt_bias, g_out_a, zero_qkv, zero_s,
                                                prompt=True, bsz=bp, t=t)
            qcp.append(buf)
            sp.append(s_fin)
            og_s, buf, s_fin = _delta_mixer_mid(proj_s, ab_s, l, conv_a, a_log, dt_bias, g_out_a, state_qkv_conv[l],
                                                state_delta[l], prompt=False, bsz=bs, t=1)
            qcs.append(buf)
            ss.append(s_fin)
            xp, xs = mm(og_p, og_s, w_out_a, layer=l, n_out=d, res=xp, res_s=xs, k_gate=2, name="w_out", **resid)
        else:
            j = l - N_A
            lam_init = _lambda_init(l)
            q_p, q_s = mm(hp, hs, w_q_b, layer=j, n_out=d, epi="rope", rope=rope_p, rope_s=rope_s,
                          scale=B_DH ** -0.5 * math.log2(math.e), scale_s=B_DH ** -0.5, out_dtype=BF16, name="w_q")
            o_p = _attention(q_p, k_att, v_att, lam_b, g_sub_b, j, lam_init, batch=bp, seq=t, tq=min(512, t))
            o_s = _decode_attention(q_s, k_s, v_s, pages_k, pages_v, page_table, lam_b, g_sub_b, j, lam_init)
            xp, xs = mm(o_p, o_s, w_o_b, layer=j, n_out=d, res=xp, res_s=xs, k_gate=2, name="w_o", **resid)
        hp, hs = norm(xp, xs, g_ffn, l, md_p, md_s, 3, 4)
        a_p, nbuf, a_s, gate_s = _ffn1(
            hp, hs, w_gate, w_up, conv_f, b_conv_f, l, jnp.pad(zero_ffn, ((0, 0), (16 - (FFN_CONV_W - 1), 0), (0, 0))),
            jnp.swapaxes(state_ffn_conv[l], 0, 1), tm=tm, tn=256, rows_per_seq=t)
        fcp.append(nbuf[t // tm - 1::t // tm, 8 - (FFN_CONV_W - 1):])
        fcs.append(jnp.stack([state_ffn_conv[l][:, 1], gate_s], axis=1))
        xp, xs = mm(a_p, a_s, w_down, layer=l, n_out=d, tm=tm_dn, tn=256, res=xp, res_s=xs, k_gate=5,
                    name="w_down", **resid)
        if l == N_A - 1:
            mkv_p, mkv_s = mod3_p(mod_kv), mod3_s(mod_kv)
            hp, hs = norm(xp, xs, g_kv, 0, mkv_p, mkv_s, 0, 1)
            (k_p, k_att), k_s = mm(hp, hs, w_k, n_out=d, epi="rope", rope=rope_p, rope_s=rope_s,
                                   out_dtype=(F32, BF16), name="w_k")
            (v_p, v_att), v_s = mm(hp, hs, w_v, n_out=d, out_dtype=(F32, BF16), name="w_v")
    yp = _final_norm(xp.reshape(bp, t, d), g_final, tm_n)
    ys = _final_norm(xs.reshape(1, bs, d), g_final, bs).reshape(bs, 1, d)
    kv4 = lambda a, b, w: a.reshape(b, -1, B_HEADS, w)
    return (yp, ys, kv4(k_p, bp, 2 * B_DH), kv4(v_p, bp, B_DV), kv4(k_s, bs, 2 * B_DH), kv4(v_s, bs, B_DV),
            jnp.stack(sp), jnp.stack(ss), jnp.stack(qcp), jnp.stack(qcs), jnp.stack(fcp), jnp.stack(fcs))
```

```python
import functools
import math

import jax
import jax.numpy as jnp
from jax import lax
from jax.experimental import pallas as pl
from jax.experimental.pallas import tpu as pltpu

F32 = jnp.float32
BF16 = jnp.bfloat16
DT_KK = BF16
DT_STATE = BF16

D_MODEL = 4096
DEPTH = 4
N_A = DEPTH // 2
A_HEADS = 32
A_DK = 128
A_DV = 128
A_DQK = A_HEADS * A_DK
A_DVW = A_HEADS * A_DV
QKV_W = 2 * A_DQK + A_DVW
CONV_W = 4
CHUNK = 64
B_HEADS = 16
B_DH = 128
B_DV = 256
ROPE_THETA = 10000.0
D_FF = 11008
FFN_CONV_W = 3
EPS = 1e-6
PAGE_SIZE = 128

VMEM_LIMIT_BYTES = 60000 * 1024
LANES = 128
FFN_SUB_ROWS = 1024


def _params(sem):
    return pltpu.CompilerParams(dimension_semantics=sem, vmem_limit_bytes=VMEM_LIMIT_BYTES)


def _dot(a, b):
    return jnp.dot(a, b, preferred_element_type=F32)


def _dot_f32(a, b):
    return jnp.dot(a, b, preferred_element_type=F32, precision=lax.Precision.HIGHEST)


def _dot_nt(a, b):
    return lax.dot_general(a, b, (((1,), (1,)), ((), ())), preferred_element_type=F32)


def _dot_tn(a, b):
    return lax.dot_general(a, b, (((0,), (0,)), ((), ())), preferred_element_type=F32)


def _silu(x):
    return x * jax.nn.sigmoid(x)


def _panel_spec(tm, k):
    return pl.BlockSpec((tm, k), lambda i, j: (i, 0), pipeline_mode=pl.Buffered(1))


def _mod_body(c_ref, w_ref, b_ref, o_ref):
    c = c_ref[...]
    o_ref[...] = _dot(_silu(c).astype(BF16), w_ref[...].astype(BF16)) + b_ref[...]


def _mod_all(c_all, w, b, tn=512):
    nl, k, n = w.shape
    mp = c_all.shape[0]
    return pl.pallas_call(
        _mod_body,
        grid=(nl, n // tn),
        in_specs=[
            pl.BlockSpec((mp, k), lambda l, j: (0, 0)),
            pl.BlockSpec((None, k, tn), lambda l, j: (l, 0, j)),
            pl.BlockSpec((None, 1, tn), lambda l, j: (l, 0, j)),
        ],
        out_specs=pl.BlockSpec((None, mp, tn), lambda l, j: (l, 0, j)),
        out_shape=jax.ShapeDtypeStruct((nl, mp, n), F32),
        compiler_params=_params(("parallel", "parallel")),
        name="adaln_mod",
    )(c_all, w, b.reshape(nl, 1, n))


def _norm_mod_body(x_ref, g_ref, sh_ref, sc_ref, o_ref):
    x = x_ref[...]
    y = x * lax.rsqrt(jnp.mean(x * x, axis=-1, keepdims=True) + EPS) * g_ref[...]
    o_ref[...] = (y * (1.0 + sc_ref[...]) + sh_ref[...]).astype(o_ref.dtype)


def _norm_body(x_ref, g_ref, o_ref):
    x = x_ref[...]
    o_ref[...] = (x * lax.rsqrt(jnp.mean(x * x, axis=-1, keepdims=True) + EPS) * g_ref[...]).astype(o_ref.dtype)


def _norm_mod(x3, g, layer, mod3, k_shift, k_scale, tm, out_dtype):
    b, t, d = x3.shape
    g3 = g.reshape(-1, 1, d)
    r = mod3.shape[1]
    rb = 1 if r == 1 else tm
    mspec = lambda k: pl.BlockSpec((None, rb, d), lambda bi, i: (bi, 0 if r == 1 else i, k))
    return pl.pallas_call(
        _norm_mod_body,
        grid=(b, t // tm),
        in_specs=[
            pl.BlockSpec((None, tm, d), lambda bi, i: (bi, i, 0)),
            pl.BlockSpec((None, 1, d), lambda bi, i: (layer, 0, 0)),
            mspec(k_shift),
            mspec(k_scale),
        ],
        out_specs=pl.BlockSpec((None, tm, d), lambda bi, i: (bi, i, 0)),
        out_shape=jax.ShapeDtypeStruct((b, t, d), out_dtype),
        compiler_params=_params(("parallel", "parallel")),
        name="norm_mod",
    )(x3, g3, mod3, mod3)


def _final_norm(x3, g, tm):
    b, t, d = x3.shape
    return pl.pallas_call(
        _norm_body,
        grid=(b, t // tm),
        in_specs=[
            pl.BlockSpec((None, tm, d), lambda bi, i: (bi, i, 0)),
            pl.BlockSpec((1, d), lambda bi, i: (0, 0)),
        ],
        out_specs=pl.BlockSpec((None, tm, d), lambda bi, i: (bi, i, 0)),
        out_shape=jax.ShapeDtypeStruct((b, t, d), F32),
        compiler_params=_params(("parallel", "parallel")),
        name="final_norm",
    )(x3, g.reshape(1, d))


def _epilogue(acc, epi, refs, scale):
    if epi == "plain":
        return acc
    if epi == "resid":
        res_ref, gt_ref = refs
        return res_ref[...] + gt_ref[...] * acc
    cs = refs[0][...]
    sn = refs[1][...]
    parts = []
    for gi in range(acc.shape[1] // LANES):
        a = acc[:, gi * LANES:(gi + 1) * LANES]
        parts.append(a * cs + pltpu.roll(a, LANES // 2, axis=1) * sn)
    out = jnp.concatenate(parts, axis=1)
    return out * scale if scale != 1.0 else out


def _mm_body(x_ref, w_ref, xs_ref, *rest, epi, scale, scale_s, n_outs, w_t):
    n_epi = 0 if epi == "plain" else 2
    epi_refs, epi_refs_s = rest[:n_epi], rest[n_epi:2 * n_epi]
    o_refs = rest[2 * n_epi:2 * n_epi + n_outs]
    os_ref = rest[2 * n_epi + n_outs]
    dot = _dot_nt if w_t else _dot
    wb = w_ref[...].astype(BF16)
    out = _epilogue(dot(x_ref[...].astype(BF16), wb), epi, epi_refs, scale)
    for o_ref in o_refs:
        o_ref[...] = out.astype(o_ref.dtype)

    @pl.when(pl.program_id(0) == 0)
    def _():
        os_ref[...] = _epilogue(dot(xs_ref[...].astype(BF16), wb), epi, epi_refs_s, scale_s)

    @pl.when(pl.program_id(0) != 0)
    def _():
        os_ref[...] = jnp.zeros(os_ref.shape, F32)


def _matmul(x, xs, w, *, layer=0, col_off=0, n_out, tm, tn, rows_per_seq, w_t=False, epi="plain", res=None,
            res_s=None, mod3=None, mod3_s=None, k_gate=0, rope=None, rope_s=None, scale=1.0, scale_s=1.0,
            out_dtype=F32, name="matmul"):
    m, k = x.shape
    ms = xs.shape[0]
    tps = rows_per_seq // tm
    off = col_off // tn
    if w_t:
        wspec = pl.BlockSpec((None, tn, k), lambda i, j: (layer, j + off, 0))
    elif w.ndim == 3:
        wspec = pl.BlockSpec((None, k, tn), lambda i, j: (layer, 0, j + off))
    else:
        wspec = pl.BlockSpec((k, tn), lambda i, j: (0, j + off))
    in_specs = [_panel_spec(tm, k), wspec, pl.BlockSpec((ms, k), lambda i, j: (0, 0))]
    args = [x, w, xs]
    if epi == "resid":
        goff = k_gate * n_out // tn
        in_specs += [
            pl.BlockSpec((tm, tn), lambda i, j: (i, j)),
            pl.BlockSpec((None, 1, tn), lambda i, j: (i // tps, 0, goff + j)),
            pl.BlockSpec((ms, tn), lambda i, j: (0, j)),
            pl.BlockSpec((None, ms, tn), lambda i, j: (0, 0, goff + j)),
        ]
        args += [res, mod3, res_s, mod3_s]
    elif epi == "rope":
        in_specs += [pl.BlockSpec((tm, LANES), lambda i, j: (i % tps, 0))] * 2
        in_specs += [pl.BlockSpec((ms, LANES), lambda i, j: (0, 0))] * 2
        args += list(rope) + list(rope_s)
    dtypes = out_dtype if isinstance(out_dtype, tuple) else (out_dtype,)
    *outs, out_s = pl.pallas_call(
        functools.partial(_mm_body, epi=epi, scale=scale, scale_s=scale_s, n_outs=len(dtypes), w_t=w_t),
        grid=(m // tm, n_out // tn),
        in_specs=in_specs,
        out_specs=[pl.BlockSpec((tm, tn), lambda i, j: (i, j))] * len(dtypes)
        + [pl.BlockSpec((None, ms, tn), lambda i, j: (i, 0, j))],
        out_shape=[jax.ShapeDtypeStruct((m, n_out), dt) for dt in dtypes]
        + [jax.ShapeDtypeStruct((m // tm, ms, n_out), F32)],
        compiler_params=_params(("arbitrary", "arbitrary")),
        name=name,
    )(*args)
    return (tuple(outs) if isinstance(out_dtype, tuple) else outs[0]), out_s[0]


def _ffn1_body(x_ref, wg_ref, wu_ref, cw_ref, cb_ref, p_ref, xs_ref, ps_ref, a_ref, nb_ref, as_ref, gs_ref,
               carry_ref, *, tps):
    i = pl.program_id(0)
    j = pl.program_id(1)
    wg = wg_ref[...].astype(BF16)
    wu = wu_ref[...].astype(BF16)
    cw = cw_ref[...]
    cb = cb_ref[...]
    tm, tn = a_ref.shape

    @pl.when(i == 0)
    def _():
        carry_ref[j] = jnp.zeros((16, tn), F32)
        xs = xs_ref[...].astype(BF16)
        g = _dot(xs, wg)
        gs_ref[...] = g
        y = cw[2:3] * g + cw[1:2] * ps_ref[1] + cw[0:1] * ps_ref[0] + cb
        as_ref[...] = _silu(y) * _dot(xs, wu)

    @pl.when(i != 0)
    def _():
        gs_ref[...] = jnp.zeros(gs_ref.shape, F32)
        as_ref[...] = jnp.zeros(as_ref.shape, F32)

    prev = jnp.where((i % tps) == 0, p_ref[...], carry_ref[j])
    r16 = lax.broadcasted_iota(jnp.int32, (16, tn), 0)
    sub = min(tm, FFN_SUB_ROWS)
    for s in range(tm // sub):
        rs = slice(s * sub, (s + 1) * sub)
        x = x_ref[rs, :].astype(BF16)
        g = _dot(x, wg)
        u = _dot(x, wu)
        head = g[0:16]

        def shifted(k):
            h16 = jnp.where(r16 < k, pltpu.roll(prev, k, axis=0), pltpu.roll(head, k, axis=0))
            return jnp.concatenate([h16, pltpu.roll(g, k, axis=0)[16:]], axis=0)

        y = cw[2:3] * g + cw[1:2] * shifted(1) + cw[0:1] * shifted(2) + cb
        a_ref[rs, :] = (_silu(y) * u).astype(a_ref.dtype)
        prev = g[sub - 16:sub]
    carry_ref[j] = prev
    nb_ref[...] = prev[8:16]


def _ffn1(h, hs, w_gate, w_up, conv_f, b_conv_f, layer, prev, prev_s, *, tm, tn, rows_per_seq):
    m, k = h.shape
    ms = hs.shape[0]
    n = D_FF
    tps = rows_per_seq // tm
    sspec = pl.BlockSpec((None, ms, tn), lambda i, j: (i, 0, j))
    a, nb, a_s, g_s = pl.pallas_call(
        functools.partial(_ffn1_body, tps=tps),
        grid=(m // tm, n // tn),
        in_specs=[
            _panel_spec(tm, k),
            pl.BlockSpec((None, k, tn), lambda i, j: (layer, 0, j)),
            pl.BlockSpec((None, k, tn), lambda i, j: (layer, 0, j)),
            pl.BlockSpec((None, FFN_CONV_W, tn), lambda i, j: (layer, 0, j)),
            pl.BlockSpec((None, 1, tn), lambda i, j: (layer, 0, j)),
            pl.BlockSpec((None, 16, tn), lambda i, j: (i // tps, 0, j)),
            pl.BlockSpec((ms, k), lambda i, j: (0, 0)),
            pl.BlockSpec((2, ms, tn), lambda i, j: (0, 0, j)),
        ],
        out_specs=[
            pl.BlockSpec((tm, tn), lambda i, j: (i, j)),
            pl.BlockSpec((None, 8, tn), lambda i, j: (i, 0, j)),
            sspec,
            sspec,
        ],
        out_shape=[
            jax.ShapeDtypeStruct((m, n), BF16),
            jax.ShapeDtypeStruct((m // tm, 8, n), F32),
            jax.ShapeDtypeStruct((m // tm, ms, n), F32),
            jax.ShapeDtypeStruct((m // tm, ms, n), F32),
        ],
        scratch_shapes=[pltpu.VMEM((n // tn, 16, tn), F32)],
        compiler_params=_params(("arbitrary", "arbitrary")),
        name="ffn_gate_up",
    )(h, w_gate, w_up, conv_f, b_conv_f.reshape(DEPTH, 1, n), prev, hs, prev_s)
    return a, nb, a_s[0], g_s[0]


def _qkvconv_body(x_ref, cw_ref, p_ref, o_ref, *scratch, tps, seq_rows, q_blocks, qk_blocks):
    jn = pl.program_id(0)
    i = pl.program_id(1)
    x = x_ref[...]
    cw = cw_ref[...]
    tm, tn = x.shape
    if seq_rows:
        (carry_ref,) = scratch

        @pl.when(i == 0)
        def _():
            carry_ref[...] = jnp.zeros((8, tn), F32)

        prev = jnp.where((i % tps) == 0, p_ref[...], carry_ref[...])
        r8 = lax.broadcasted_iota(jnp.int32, (8, tn), 0)
        head = x[0:8]

        def shifted(s):
            h8 = jnp.where(r8 < s, pltpu.roll(prev, s, axis=0), pltpu.roll(head, s, axis=0))
            return jnp.concatenate([h8, pltpu.roll(x, s, axis=0)[8:]], axis=0)

        x1, x2, x3 = shifted(1), shifted(2), shifted(3)
        carry_ref[...] = x[tm - 8:tm]
    else:
        x1, x2, x3 = p_ref[2], p_ref[1], p_ref[0]
    y = _silu(cw[3:4] * x + cw[2:3] * x1 + cw[1:2] * x2 + cw[0:1] * x3)
    mult = jnp.where(jn < q_blocks, A_DK ** -0.5, 1.0)
    is_qk = jn < qk_blocks
    parts = []
    for gi in range(tn // LANES):
        z = y[:, gi * LANES:(gi + 1) * LANES]
        nrm = lax.rsqrt(jnp.sum(z * z, axis=-1, keepdims=True) + EPS) * mult
        parts.append(z * jnp.where(is_qk, nrm, 1.0))
    o_ref[...] = jnp.concatenate(parts, axis=1)


def _qkvconv(proj, conv_a, layer, prev, *, tm, tn, rows_per_seq, seq_rows):
    m = proj.shape[0]
    tps = rows_per_seq // tm
    if seq_rows:
        pspec = pl.BlockSpec((None, 8, tn), lambda jn, i: (i // tps, 0, jn))
        scratch = [pltpu.VMEM((8, tn), F32)]
    else:
        pspec = pl.BlockSpec((CONV_W - 1, m, tn), lambda jn, i: (0, 0, jn))
        scratch = []
    return pl.pallas_call(
        functools.partial(_qkvconv_body, tps=tps, seq_rows=seq_rows, q_blocks=A_DQK // tn,
                          qk_blocks=2 * A_DQK // tn),
        grid=(QKV_W // tn, m // tm),
        in_specs=[
            pl.BlockSpec((tm, tn), lambda jn, i: (i, jn)),
            pl.BlockSpec((None, CONV_W, tn), lambda jn, i: (layer, 0, jn)),
            pspec,
        ],
        out_specs=pl.BlockSpec((tm, tn), lambda jn, i: (i, jn)),
        out_shape=jax.ShapeDtypeStruct((m, QKV_W), F32),
        scratch_shapes=scratch,
        compiler_params=_params(("arbitrary", "arbitrary")),
        name="qkv_conv",
    )(proj, conv_a, prev)


def _gates_body(ab_ref, alog_ref, dtb_ref, g_ref, beta_ref, *, cumsum):
    ab = ab_ref[...]
    x = ab + dtb_ref[...]
    softplus = jnp.maximum(x, 0.0) + jnp.log1p(jnp.exp(-jnp.abs(x)))
    g = -jnp.exp(alog_ref[...]) * softplus
    if cumsum:
        r = lax.broadcasted_iota(jnp.int32, g.shape, 0) & (CHUNK - 1)
        s = 1
        while s < CHUNK:
            g = g + jnp.where(r >= s, pltpu.roll(g, s, axis=0), 0.0)
            s *= 2
    g_ref[...] = g
    beta_ref[...] = jax.nn.sigmoid(ab)


def _gates(ab, a_log, dt_bias, *, tm, cumsum):
    m, width = ab.shape
    pad = lambda v: jnp.pad(v.astype(F32), (0, width - A_HEADS)).reshape(1, width)
    return pl.pallas_call(
        functools.partial(_gates_body, cumsum=cumsum),
        grid=(m // tm,),
        in_specs=[
            pl.BlockSpec((tm, width), lambda i: (i, 0)),
            pl.BlockSpec((1, width), lambda i: (0, 0)),
            pl.BlockSpec((1, width), lambda i: (0, 0)),
        ],
        out_specs=[pl.BlockSpec((tm, width), lambda i: (i, 0))] * 2,
        out_shape=[jax.ShapeDtypeStruct((m, width), F32)] * 2,
        compiler_params=_params(("parallel",)),
        name="delta_gates",
    )(ab, pad(a_log), pad(dt_bias))


def _delta_body(q_ref, k_ref, v_ref, gate_ref, gc_ref, gct_ref, beta_ref, s0_ref, gout_ref, og_ref, s_ref, *,
                heads, chunks):
    @pl.when(pl.program_id(2) == 0)
    def _():
        s_ref[...] = s0_ref[...]

    c = CHUNK
    ii = lax.broadcasted_iota(jnp.int32, (c, c), 0)
    jj = lax.broadcasted_iota(jnp.int32, (c, c), 1)
    eye = jnp.where(ii == jj, 1.0, 0.0)
    gout = gout_ref[...]
    units = [(ci, hh) for ci in range(chunks) for hh in range(heads)]
    rows = lambda ci: slice(ci * c, (ci + 1) * c)
    cols = lambda hh: slice(hh * A_DK, (hh + 1) * A_DK)
    q = [q_ref[rows(ci), cols(hh)] for ci, hh in units]
    k = [k_ref[rows(ci), cols(hh)] for ci, hh in units]
    gcol = [gc_ref[rows(ci), hh:hh + 1] for ci, hh in units]
    bcol = [beta_ref[rows(ci), hh:hh + 1] for ci, hh in units]
    kb = [x.astype(DT_KK) for x in k]
    kk = [_dot_nt(x, x) for x in kb]
    qk = [_dot_nt(x.astype(DT_KK), y) for x, y in zip(q, kb)]
    a_kk, a_qk, rhs, eg = [], [], [], []
    for u, (ci, hh) in enumerate(units):
        dec = jnp.exp(jnp.where(ii >= jj, gcol[u] - gct_ref[hh:hh + 1, rows(ci)], -jnp.inf))
        a_kk.append(jnp.where(ii > jj, kk[u] * dec, 0.0) * bcol[u])
        a_qk.append((qk[u] * dec).astype(DT_STATE))
        eg.append(jnp.exp(gcol[u]))
        rhs.append(jnp.concatenate([v_ref[rows(ci), cols(hh)] * bcol[u], k[u] * (bcol[u] * eg[u])], axis=1))
    pw = [(-a).astype(BF16) for a in a_kk]
    t_inv = [eye - a for a in a_kk]
    for _ in range(5):
        pw = [_dot(p, p).astype(BF16) for p in pw]
        t_inv = [t + _dot(t.astype(BF16), p) for t, p in zip(t_inv, pw)]
    tb = [t.astype(BF16) for t in t_inv]
    y0 = [_dot(t, r.astype(BF16)) for t, r in zip(tb, rhs)]
    y_hi = [y.astype(BF16) for y in y0]
    a_hi = [a.astype(BF16) for a in a_kk]
    ay = [_dot(a, jnp.concatenate([yh, (y - yh.astype(F32)).astype(BF16)], axis=1))
          for a, y, yh in zip(a_hi, y0, y_hi)]
    al = [_dot((a - ah.astype(F32)).astype(BF16), yh) for a, ah, yh in zip(a_kk, a_hi, y_hi)]
    resid = [r - y - (p[:, :2 * A_DV] + p[:, 2 * A_DV:] + l) for r, y, p, l in zip(rhs, y0, ay, al)]
    ysol = [y + _dot(t, r.astype(BF16)) for y, t, r in zip(y0, tb, resid)]
    for ci in range(chunks):
        us = [u for u, (cu, _) in enumerate(units) if cu == ci]
        s_old = [s_ref[units[u][1]] for u in us]
        sb = [x.astype(DT_STATE) for x in s_old]
        w_s = [_dot(ysol[u][:, A_DV:].astype(DT_STATE), x) for u, x in zip(us, sb)]
        q_s = [_dot((q[u] * eg[u]).astype(DT_STATE), x) for u, x in zip(us, sb)]
        vnb = [(ysol[u][:, :A_DV] - w).astype(DT_STATE) for u, w in zip(us, w_s)]
        glast = [gc_ref[ci * c + c - 1:ci * c + c, units[u][1]:units[u][1] + 1] for u in us]
        kdec = [(k[u] * jnp.exp(g - gcol[u])).astype(DT_STATE) for u, g in zip(us, glast)]
        o_s = [x + _dot(a_qk[u], vn) for u, x, vn in zip(us, q_s, vnb)]
        s_up = [_dot_tn(kd, vn) for kd, vn in zip(kdec, vnb)]
        for n, u in enumerate(us):
            hh = units[u][1]
            s_ref[hh] = jnp.exp(glast[n]) * s_old[n] + s_up[n]
            o = o_s[n]
            on = o * lax.rsqrt(jnp.mean(o * o, axis=-1, keepdims=True) + EPS) * gout
            og_ref[rows(ci), cols(hh)] = (on * _silu(gate_ref[rows(ci), cols(hh)])).astype(og_ref.dtype)


def _delta(qkv, proj, gc_g, gct_g, beta_g, s0, g_out, layer, *, batch, seq, heads=4, chunks=2):
    m = qkv.shape[0]
    rows = chunks * CHUNK
    nsteps = seq // rows
    w = heads * A_DK
    qo, ko, vo, go = 0, A_DQK // w, 2 * A_DQK // w, QKV_W // w
    blk = lambda off: pl.BlockSpec((rows, w), lambda b, hg, t: (b * nsteps + t, off + hg))
    return pl.pallas_call(
        functools.partial(_delta_body, heads=heads, chunks=chunks),
        grid=(batch, A_HEADS // heads, nsteps),
        in_specs=[
            blk(qo), blk(ko), blk(vo), blk(go),
            pl.BlockSpec((None, rows, heads), lambda b, hg, t: (hg, b * nsteps + t, 0)),
            pl.BlockSpec((None, heads, rows), lambda b, hg, t: (hg, 0, b * nsteps + t)),
            pl.BlockSpec((None, rows, heads), lambda b, hg, t: (hg, b * nsteps + t, 0)),
            pl.BlockSpec((None, heads, A_DK, A_DV), lambda b, hg, t: (b, hg, 0, 0)),
            pl.BlockSpec((None, 1, A_DV), lambda b, hg, t: (layer, 0, 0)),
        ],
        out_specs=[
            pl.BlockSpec((rows, w), lambda b, hg, t: (b * nsteps + t, hg)),
            pl.BlockSpec((None, heads, A_DK, A_DV), lambda b, hg, t: (b, hg, 0, 0)),
        ],
        out_shape=[
            jax.ShapeDtypeStruct((m, A_DVW), BF16),
            jax.ShapeDtypeStruct((batch, A_HEADS, A_DK, A_DV), F32),
        ],
        compiler_params=_params(("parallel", "parallel", "arbitrary")),
        name="delta_rule",
    )(qkv, qkv, qkv, proj, gc_g, gct_g, beta_g, s0, g_out.reshape(N_A, 1, A_DV))


def _lambda(lam_ref, lam_init):
    lv = lam_ref[...]
    return (jnp.exp(jnp.sum(lv[0:1] * lv[1:2], axis=-1, keepdims=True))
            - jnp.exp(jnp.sum(lv[2:3] * lv[3:4], axis=-1, keepdims=True)) + lam_init)


def _attn_body(qt_ref, kt_ref, q_ref, k_ref, v_ref, lam_ref, gsub_ref, o_ref, m_ref, l_ref, acc_ref, *, tq,
               lam_init):
    qi = qt_ref[pl.program_id(2)]
    ki = kt_ref[pl.program_id(2)]

    @pl.when(ki == 0)
    def _():
        m_ref[...] = jnp.full(m_ref.shape, -jnp.inf, F32)
        l_ref[...] = jnp.zeros(l_ref.shape, F32)
        acc_ref[...] = jnp.zeros(acc_ref.shape, F32)

    def step(diagonal):
        q = q_ref[...].astype(BF16)
        k = k_ref[...].astype(BF16)
        v = v_ref[...].astype(BF16)
        if diagonal:
            mask = (lax.broadcasted_iota(jnp.int32, (tq, tq), 1) <= lax.broadcasted_iota(jnp.int32, (tq, tq), 0))
        for mi in range(2):
            sl = slice(mi * B_DH, (mi + 1) * B_DH)
            s = _dot_nt(q[:, sl], k[:, sl])
            if diagonal:
                s = jnp.where(mask, s, -jnp.inf)
            m_old = m_ref[mi]
            m_new = jnp.maximum(m_old, jnp.max(s, axis=-1, keepdims=True))
            alpha = jnp.exp2(m_old - m_new)
            p = jnp.exp2(s - m_new)
            l_ref[mi] = alpha * l_ref[mi] + jnp.sum(p, axis=-1, keepdims=True)
            acc_ref[mi] = alpha * acc_ref[mi] + _dot(p.astype(BF16), v)
            m_ref[mi] = m_new

    @pl.when(ki < qi)
    def _():
        step(False)

    @pl.when(ki == qi)
    def _():
        step(True)
        lam = _lambda(lam_ref, lam_init)
        o = acc_ref[0] / l_ref[0] - lam * (acc_ref[1] / l_ref[1])
        o = o * lax.rsqrt(jnp.mean(o * o, axis=-1, keepdims=True) + EPS) * gsub_ref[...]
        o_ref[...] = (o * (1.0 - lam_init)).astype(o_ref.dtype)


def _attention(q, k, v, lam_b, g_sub, j, lam_init, *, batch, seq, tq=512):
    m = q.shape[0]
    nq = seq // tq
    w = 2 * B_DH
    pairs = [(qi, ki) for qi in range(nq) for ki in range(qi + 1)]
    q_tab = jnp.array([p[0] for p in pairs], jnp.int32)
    k_tab = jnp.array([p[1] for p in pairs], jnp.int32)
    grid_spec = pltpu.PrefetchScalarGridSpec(
        num_scalar_prefetch=2,
        grid=(batch, B_HEADS, len(pairs)),
        in_specs=[
            pl.BlockSpec((tq, w), lambda b, h, s, qt, kt: (b * nq + qt[s], h)),
            pl.BlockSpec((tq, w), lambda b, h, s, qt, kt: (b * nq + kt[s], h)),
            pl.BlockSpec((tq, w), lambda b, h, s, qt, kt: (b * nq + kt[s], h)),
            pl.BlockSpec((None, 4, B_DH), lambda b, h, s, qt, kt: (j, 0, 0)),
            pl.BlockSpec((None, 1, B_DV), lambda b, h, s, qt, kt: (j, 0, 0)),
        ],
        out_specs=pl.BlockSpec((tq, w), lambda b, h, s, qt, kt: (b * nq + qt[s], h)),
        scratch_shapes=[
            pltpu.VMEM((2, tq, 1), F32),
            pltpu.VMEM((2, tq, 1), F32),
            pltpu.VMEM((2, tq, B_DV), F32),
        ],
    )
    return pl.pallas_call(
        functools.partial(_attn_body, tq=tq, lam_init=lam_init),
        grid_spec=grid_spec,
        out_shape=jax.ShapeDtypeStruct((m, B_HEADS * B_DV), BF16),
        compiler_params=_params(("parallel", "parallel", "arbitrary")),
        name="diff_attention",
    )(q_tab, k_tab, q, k, v, lam_b, g_sub.reshape(-1, 1, B_DV))


QROWS = 16


def _decode_attn_body(pt_ref, q_ref, kn_ref, vn_ref, lam_ref, gsub_ref, *refs, pages, lam_init):
    k_refs = (refs[:pages], refs[pages:2 * pages])
    v_refs = (refs[2 * pages:3 * pages], refs[3 * pages:4 * pages])
    o_ref, qbd_ref, m_ref, l_ref, acc_ref = refs[4 * pages:]
    s = pl.program_id(1)
    rr = lax.broadcasted_iota(jnp.int32, (QROWS, B_DV), 0)
    cc = lax.broadcasted_iota(jnp.int32, (QROWS, B_DV), 1)
    hcols = lambda h: slice(h * B_DV, (h + 1) * B_DV)

    @pl.when(s == 0)
    def _():
        for h in range(B_HEADS):
            qbd_ref[h] = jnp.where(lax.shift_right_logical(cc, 7) == rr, q_ref[:, hcols(h)], 0.0).astype(BF16)
        m_ref[...] = jnp.full(m_ref.shape, -jnp.inf, F32)
        l_ref[...] = jnp.zeros(l_ref.shape, F32)
        acc_ref[...] = jnp.zeros(acc_ref.shape, F32)

    head_rows = lambda h: pl.ds(h, PAGE_SIZE, stride=B_HEADS)
    halves = (slice(0, B_DH), slice(B_DH, 2 * B_DH))
    for p in range(pages):
        sc = [sum(_dot_nt(qbd_ref[h, :, halves[x]], k_refs[x][p][head_rows(h), :].astype(BF16)) for x in (0, 1))
              for h in range(B_HEADS)]
        for h in range(B_HEADS):
            m_old = m_ref[h]
            m_new = jnp.maximum(m_old, jnp.max(sc[h], axis=-1, keepdims=True))
            alpha = jnp.exp(m_old - m_new)
            pr = jnp.exp(sc[h] - m_new)
            l_ref[h] = alpha * l_ref[h] + jnp.sum(pr, axis=-1, keepdims=True)
            pb = pr.astype(BF16)
            for x in (0, 1):
                acc_ref[h, :, halves[x]] = (alpha * acc_ref[h, :, halves[x]]
                                            + _dot(pb, v_refs[x][p][head_rows(h), :].astype(BF16)))
            m_ref[h] = m_new

    @pl.when(s == pl.num_programs(1) - 1)
    def _():
        lam = _lambda(lam_ref, lam_init)
        for h in range(B_HEADS):
            kn = kn_ref[:, hcols(h)].astype(BF16).astype(F32)
            vn = vn_ref[:, hcols(h)].astype(BF16).astype(F32)
            s_new = jnp.sum(qbd_ref[h].astype(F32) * kn, axis=-1, keepdims=True)
            m_old = m_ref[h]
            m_new = jnp.maximum(m_old, s_new)
            alpha = jnp.exp(m_old - m_new)
            pn = jnp.exp(s_new - m_new)
            on = (alpha * acc_ref[h] + pn.astype(BF16).astype(F32) * vn) / (alpha * l_ref[h] + pn)
            o = on[0:1] - lam * on[1:2]
            o = o * lax.rsqrt(jnp.mean(o * o, axis=-1, keepdims=True) + EPS) * gsub_ref[...]
            o_ref[:, hcols(h)] = o * (1.0 - lam_init)


def _decode_attention(q, k_new, v_new, cache_k, cache_v, page_table, lam_b, g_sub, j, lam_init, *, pages=4):
    bsz, width = q.shape
    n_pages = page_table.shape[1]
    nsteps = n_pages // pages
    row3 = lambda a: a.reshape(bsz, 1, width)
    rspec = pl.BlockSpec((None, 1, width), lambda b, s, pt: (b, 0, 0))

    def page_spec(p, half):
        return pl.BlockSpec((None, PAGE_SIZE * B_HEADS, B_DH),
                            lambda b, s, pt: (pt[b * n_pages + s * pages + p], 0, half))

    page_specs = [page_spec(p, half) for half in (0, 1) for p in range(pages)]

    grid_spec = pltpu.PrefetchScalarGridSpec(
        num_scalar_prefetch=1,
        grid=(bsz, nsteps),
        in_specs=[
            rspec, rspec, rspec,
            pl.BlockSpec((None, 4, B_DH), lambda b, s, pt: (j, 0, 0)),
            pl.BlockSpec((None, 1, B_DV), lambda b, s, pt: (j, 0, 0)),
        ] + page_specs * 2,
        out_specs=rspec,
        scratch_shapes=[
            pltpu.VMEM((B_HEADS, QROWS, B_DV), BF16),
            pltpu.VMEM((B_HEADS, QROWS, 1), F32),
            pltpu.VMEM((B_HEADS, QROWS, 1), F32),
            pltpu.VMEM((B_HEADS, QROWS, B_DV), F32),
        ],
    )
    out = pl.pallas_call(
        functools.partial(_decode_attn_body, pages=pages, lam_init=lam_init),
        grid_spec=grid_spec,
        out_shape=jax.ShapeDtypeStruct((bsz, 1, width), F32),
        compiler_params=_params(("parallel", "arbitrary")),
        name="decode_attention",
    )(page_table.reshape(-1), row3(q), row3(k_new), row3(v_new), lam_b, g_sub.reshape(-1, 1, B_DV),
      *([cache_k] * (2 * pages)), *([cache_v] * (2 * pages)))
    return out.reshape(bsz, width)


def _lambda_init(layer):
    return 0.8 - 0.6 * math.exp(-0.3 * layer)


def _rope_tables(pos):
    half = B_DH // 2
    inv = ROPE_THETA ** (-jnp.arange(half, dtype=F32) / half)
    ang = pos.astype(F32)[:, None] * inv
    cos, sin = jnp.cos(ang), jnp.sin(ang)
    return jnp.concatenate([cos, cos], axis=-1), jnp.concatenate([-sin, sin], axis=-1)


def _group_heads(a, heads):
    m = a.shape[0]
    return a.reshape(m, A_HEADS // heads, heads).transpose(1, 0, 2)


def _delta_mixer_mid(proj, ab, l, conv_a, a_log, dt_bias, g_out_a, qkv_buf, s0, *, prompt, bsz, t):
    heads = min(8, A_HEADS)
    m = proj.shape[0]
    if prompt:
        prev = jnp.pad(qkv_buf, ((0, 0), (8 - (CONV_W - 1), 0), (0, 0)))
        new_buf = proj.reshape(bsz, t, -1)[:, t - (CONV_W - 1):, :QKV_W]
    else:
        prev = jnp.swapaxes(qkv_buf, 0, 1)
        new_buf = jnp.concatenate([qkv_buf[:, 1:], proj[:, None, :QKV_W]], axis=1)
    rows_per_seq = t if prompt else m
    qkv = _qkvconv(proj, conv_a, l, prev, tm=min(512, rows_per_seq), tn=512, rows_per_seq=rows_per_seq,
                   seq_rows=prompt)
    g_full, beta_full = _gates(ab, a_log[l], dt_bias[l], tm=min(512, m), cumsum=prompt)
    gc = g_full[:, :A_HEADS]
    beta = beta_full[:, A_HEADS:2 * A_HEADS]
    if prompt:
        qkv_d, proj_d, dseq = qkv, proj, t
    else:
        dseq = 2 * CHUNK
        padrows = lambda a: jnp.pad(a[:, None], ((0, 0), (0, dseq - 1), (0, 0))).reshape(bsz * dseq, -1)
        qkv_d, proj_d, beta = padrows(qkv), padrows(proj), padrows(beta)
        gc = jnp.concatenate([jnp.broadcast_to(gc[:, None], (bsz, CHUNK, A_HEADS)),
                              jnp.zeros((bsz, CHUNK, A_HEADS), F32)], axis=1).reshape(bsz * dseq, -1)
    og, s_fin = _delta(qkv_d, proj_d, _group_heads(gc, heads), jnp.swapaxes(_group_heads(gc, heads), 1, 2),
                       _group_heads(beta, heads), s0, g_out_a, l, batch=bsz, seq=dseq, heads=heads,
                       chunks=min(4, dseq // CHUNK))
    if not prompt:
        og = og.reshape(bsz, dseq, -1)[:, 0].astype(F32)
    return og, new_buf, s_fin


def kernel(x_prompt, x_sample, cache_k, cache_v, state_delta, state_qkv_conv, state_ffn_conv, page_table, c_prompt, c_sample, w_ada, b_ada, g_mix, g_ffn, w_in_a, conv_a, a_log, dt_bias, g_out_a, w_out_a, w_ada_kv, b_ada_kv, g_kv, w_k, w_v, w_q_b, lam_b, g_sub_b, w_o_b, w_gate, w_up, conv_f, b_conv_f, w_down, g_final):
    bp, t, d = x_prompt.shape
    bs = x_sample.shape[0]
    mp = bp * t
    tm, tm_n, tm_dn = min(2048, t), min(512, t), min(1024, t)
    c_all = jnp.concatenate([jnp.pad(c_prompt, ((0, 8 - bp), (0, 0))), c_sample], axis=0)
    mods = _mod_all(c_all, w_ada, b_ada)
    mod_kv = _mod_all(c_all, w_ada_kv[None], b_ada_kv[None])[0]
    mod3_p = lambda mm: mm[:bp].reshape(bp, 1, -1)
    mod3_s = lambda mm: mm[8:8 + bs].reshape(1, bs, -1)
    n_pool = cache_k.shape[0]
    past_len = page_table.shape[1] * PAGE_SIZE
    pages_k = cache_k.reshape(n_pool, PAGE_SIZE * B_HEADS, -1)
    pages_v = cache_v.reshape(n_pool, PAGE_SIZE * B_HEADS, -1)
    rope_p = _rope_tables(jnp.arange(t, dtype=jnp.int32))
    rope_s = _rope_tables(jnp.full((bs,), past_len, dtype=jnp.int32))
    zero_qkv = jnp.zeros((bp, CONV_W - 1, QKV_W), F32)
    zero_ffn = jnp.zeros((bp, FFN_CONV_W - 1, D_FF), F32)
    zero_s = jnp.zeros((bp, A_HEADS, A_DK, A_DV), state_delta.dtype)
    w_in_t = jnp.swapaxes(w_in_a, 1, 2)

    def norm(xp, xs, gain, layer, md_p, md_s, k_shift, k_scale):
        hp = _norm_mod(xp.reshape(bp, t, d), gain, layer, md_p, k_shift, k_scale, tm_n, BF16).reshape(mp, d)
        hs = _norm_mod(xs.reshape(1, bs, d), gain, layer, md_s, k_shift, k_scale, bs, F32).reshape(bs, d)
        return hp, hs

    def mm(xp, xs, w, **kw):
        return _matmul(xp, xs, w, tm=kw.pop("tm", tm), tn=kw.pop("tn", 512), rows_per_seq=t, **kw)

    xp = x_prompt.reshape(mp, d)
    xs = x_sample.reshape(bs, d)
    sp, ss, qcp, qcs, fcp, fcs = [], [], [], [], [], []
    for l in range(DEPTH):
        md_p, md_s = mod3_p(mods[l]), mod3_s(mods[l])
        resid = dict(epi="resid", mod3=md_p, mod3_s=md_s)
        hp, hs = norm(xp, xs, g_mix, l, md_p, md_s, 0, 1)
        if l < N_A:
            proj_p, proj_s = mm(hp, hs, w_in_t, layer=l, w_t=True, n_out=QKV_W + A_DVW, name="w_in")
            ab_p, ab_s = mm(hp, hs, w_in_t, layer=l, w_t=True, col_off=QKV_W + A_DVW, n_out=2 * A_HEADS,
                            tn=2 * A_HEADS, name="w_in_ab")
            og_p, buf, s_fin = _delta_mixer_mid(proj_p, ab_p, l, conv_a, a_log, dt_bias, g_out_a, zero_qkv, zero_s,
                                                prompt=True, bsz=bp, t=t)
            qcp.append(buf)
            sp.append(s_fin)
            og_s, buf, s_fin = _delta_mixer_mid(proj_s, ab_s, l, conv_a, a_log, dt_bias, g_out_a, state_qkv_conv[l],
                                                state_delta[l], prompt=False, bsz=bs, t=1)
            qcs.append(buf)
            ss.append(s_fin)
            xp, xs = mm(og_p, og_s, w_out_a, layer=l, n_out=d, res=xp, res_s=xs, k_gate=2, name="w_out", **resid)
        else:
            j = l - N_A
            lam_init = _lambda_init(l)
            q_p, q_s = mm(hp, hs, w_q_b, layer=j, n_out=d, epi="rope", rope=rope_p, rope_s=rope_s,
                          scale=B_DH ** -0.5 * math.log2(math.e), scale_s=B_DH ** -0.5, out_dtype=BF16, name="w_q")
            o_p = _attention(q_p, k_att, v_att, lam_b, g_sub_b, j, lam_init, batch=bp, seq=t, tq=min(512, t))
            o_s = _decode_attention(q_s, k_s, v_s, pages_k, pages_v, page_table, lam_b, g_sub_b, j, lam_init)
            xp, xs = mm(o_p, o_s, w_o_b, layer=j, n_out=d, res=xp, res_s=xs, k_gate=2, name="w_o", **resid)
        hp, hs = norm(xp, xs, g_ffn, l, md_p, md_s, 3, 4)
        a_p, nbuf, a_s, gate_s = _ffn1(
            hp, hs, w_gate, w_up, conv_f, b_conv_f, l, jnp.pad(zero_ffn, ((0, 0), (16 - (FFN_CONV_W - 1), 0), (0, 0))),
            jnp.swapaxes(state_ffn_conv[l], 0, 1), tm=tm, tn=256, rows_per_seq=t)
        fcp.append(nbuf[t // tm - 1::t // tm, 8 - (FFN_CONV_W - 1):])
        fcs.append(jnp.stack([state_ffn_conv[l][:, 1], gate_s], axis=1))
        xp, xs = mm(a_p, a_s, w_down, layer=l, n_out=d, tm=tm_dn, tn=256, res=xp, res_s=xs, k_gate=5,
                    name="w_down", **resid)
        if l == N_A - 1:
            mkv_p, mkv_s = mod3_p(mod_kv), mod3_s(mod_kv)
            hp, hs = norm(xp, xs, g_kv, 0, mkv_p, mkv_s, 0, 1)
            (k_p, k_att), k_s = mm(hp, hs, w_k, n_out=d, epi="rope", rope=rope_p, rope_s=rope_s,
                                   out_dtype=(F32, BF16), name="w_k")
            (v_p, v_att), v_s = mm(hp, hs, w_v, n_out=d, out_dtype=(F32, BF16), name="w_v")
    yp = _final_norm(xp.reshape(bp, t, d), g_final, tm_n)
    ys = _final_norm(xs.reshape(1, bs, d), g_final, bs).reshape(bs, 1, d)
    kv4 = lambda a, b, w: a.reshape(b, -1, B_HEADS, w)
    return (yp, ys, kv4(k_p, bp, 2 * B_DH), kv4(v_p, bp, B_DV), kv4(k_s, bs, 2 * B_DH), kv4(v_s, bs, B_DV),
            jnp.stack(sp), jnp.stack(ss), jnp.stack(qcp), jnp.stack(qcs), jnp.stack(fcp), jnp.stack(fcs))
```
